```python
import numpy as np
import jax
import jax.numpy as jnp
from jax import lax

D_MODEL = 1024
BATCH = 8
SEQ = 4096
DEPTH = 1

N_MEM = 256
EPS = 1e-6
MLA_HEADS = 8
Q_LORA = 256
KV_LORA = 128
NOPE_DIM = 64
ROPE_DIM = 32
V_DIM = 64
ROPE_THETA = 10000.0
Q_BLOCK = 128
MLSTM_HEADS = 4
MLSTM_HEAD_DIM = 128
MLSTM_WIDTH = MLSTM_HEADS * MLSTM_HEAD_DIM
MLSTM_CHUNK = 128
MLSTM_CONV = 3
MIX_WIDTH = MLA_HEADS * V_DIM + MLSTM_WIDTH
IN_SPLITS = (Q_LORA, KV_LORA, ROPE_DIM, 2 * MLSTM_WIDTH, MLSTM_WIDTH, MLSTM_WIDTH, 4 * MLSTM_HEADS)
IN_WIDTH = sum(IN_SPLITS)
X_HEADS = 4
X_HEAD_DIM = D_MODEL // X_HEADS
D_FF = ((8 * D_MODEL + 767) // 768) * 256

kernel_name = 'hymba_mla_mlstm_memory_encoder_layer'


def rms_norm(x, g):
    xf = x.astype(jnp.float32)
    y = xf * lax.rsqrt(jnp.mean(xf * xf, axis=-1, keepdims=True) + EPS)
    return (y * g.astype(jnp.float32)).astype(x.dtype)


def rope_tables(positions):
    inv = ROPE_THETA ** (-jnp.arange(0, ROPE_DIM, 2, dtype=jnp.float32) / ROPE_DIM)
    ang = positions.astype(jnp.float32)[..., None] * inv
    return jnp.cos(ang), jnp.sin(ang)


def apply_rope(x, cos, sin):
    xf = x.astype(jnp.float32)
    x1, x2 = jnp.split(xf, 2, axis=-1)
    return jnp.concatenate([x1 * cos - x2 * sin, x2 * cos + x1 * sin], axis=-1).astype(x.dtype)


def mla_attention(z_cq, z_ckv, z_kr, q_norm, w_uq, kv_norm, w_ukv, cos, sin):
    B, S, _ = z_cq.shape
    q = (rms_norm(z_cq, q_norm) @ w_uq).reshape(B, S, MLA_HEADS, NOPE_DIM + ROPE_DIM)
    q_nope = q[..., :NOPE_DIM]
    q_rope = apply_rope(q[..., NOPE_DIM:], cos[:, :, None, :], sin[:, :, None, :])
    kv = (rms_norm(z_ckv, kv_norm) @ w_ukv).reshape(B, S, MLA_HEADS, NOPE_DIM + V_DIM)
    k_nope, v = kv[..., :NOPE_DIM], kv[..., NOPE_DIM:]
    k_rope = apply_rope(z_kr, cos, sin)
    scale = (NOPE_DIM + ROPE_DIM) ** -0.5
    nb = S // Q_BLOCK
    qn = q_nope.reshape(B, nb, Q_BLOCK, MLA_HEADS, NOPE_DIM).swapaxes(0, 1)
    qr = q_rope.reshape(B, nb, Q_BLOCK, MLA_HEADS, ROPE_DIM).swapaxes(0, 1)

    def block(args):
        qn_b, qr_b = args
        s = (jnp.einsum('bqhd,bkhd->bhqk', qn_b, k_nope)
             + jnp.einsum('bqhr,bkr->bhqk', qr_b, k_rope))
        p = jax.nn.softmax(s.astype(jnp.float32) * scale, axis=-1).astype(v.dtype)
        return jnp.einsum('bhqk,bkhd->bqhd', p, v)

    o = lax.map(block, (qn, qr))
    return o.swapaxes(0, 1).reshape(B, S, MLA_HEADS * V_DIM)


def centred_dwconv(u, w):
    K = w.shape[0]
    pad = (K - 1) // 2
    S = u.shape[1]
    up = jnp.pad(u, ((0, 0), (pad, pad), (0, 0)))
    out = up[:, 0:S] * w[0]
    for j in range(1, K):
        out = out + up[:, j:j + S] * w[j]
    return out


def mlstm_direction(q, k, v, i_pre, f_pre):
    B, H, S, Dh = q.shape
    L = MLSTM_CHUNK
    NC = S // L
    qc = q.reshape(B, H, NC, L, Dh)
    kc = k.reshape(B, H, NC, L, Dh)
    vc = v.reshape(B, H, NC, L, Dh)
    logf = jax.nn.log_sigmoid(f_pre).reshape(B, H, NC, L)
    logi = i_pre.reshape(B, H, NC, L)
    b = jnp.cumsum(logf, axis=-1)
    tri = jnp.tril(jnp.ones((L, L), dtype=bool))
    log_d = jnp.where(tri, b[..., :, None] - b[..., None, :] + logi[..., None, :], -jnp.inf)
    w_end = b[..., -1:] - b + logi
    m_loc = jnp.max(w_end, axis=-1)
    e_end = jnp.exp(w_end - m_loc[..., None])
    C_loc = jnp.einsum('bhcs,bhcsd,bhcse->bhcde', e_end, vc, kc)
    n_loc = jnp.einsum('bhcs,bhcse->bhce', e_end, kc)
    b_last = b[..., -1]

    def step(carry, inp):
        C, n, m = carry
        Cl, nl, ml, bl = inp
        m_new = jnp.maximum(bl + m, ml)
        a = jnp.exp(bl + m - m_new)
        c = jnp.exp(ml - m_new)
        C_new = a[..., None, None] * C + c[..., None, None] * Cl
        n_new = a[..., None] * n + c[..., None] * nl
        return (C_new, n_new, m_new), (C, n, m)

    init = (jnp.zeros((B, H, Dh, Dh), jnp.float32),
            jnp.zeros((B, H, Dh), jnp.float32),
            jnp.zeros((B, H), jnp.float32))
    xs = (jnp.moveaxis(C_loc, 2, 0), jnp.moveaxis(n_loc, 2, 0),
          jnp.moveaxis(m_loc, 2, 0), jnp.moveaxis(b_last, 2, 0))
    _, (C_prev, n_prev, m_prev) = lax.scan(step, init, xs)
    C_prev = jnp.moveaxis(C_prev, 0, 2)
    n_prev = jnp.moveaxis(n_prev, 0, 2)
    m_prev = jnp.moveaxis(m_prev, 0, 2)

    log_inter = b + m_prev[..., None]
    m_t = jnp.maximum(log_inter, jnp.max(log_d, axis=-1))
    p = jnp.exp(log_d - m_t[..., None])
    a = jnp.exp(log_inter - m_t)
    s = jnp.einsum('bhctd,bhcsd->bhcts', qc, kc) * p
    num = (jnp.einsum('bhcts,bhcsd->bhctd', s, vc)
           + a[..., None] * jnp.einsum('bhcde,bhcte->bhctd', C_prev, qc))
    den = jnp.sum(s, axis=-1) + a * jnp.einsum('bhce,bhcte->bhct', n_prev, qc)
    h = num / jnp.maximum(jnp.abs(den), jnp.exp(-m_t))[..., None]
    return h.reshape(B, H, S, Dh)


def mlstm_mixer(z_qk, z_v, z_o, z_g, conv_w, gate_bias, out_norm):
    B, S, _ = z_qk.shape
    qk = jax.nn.silu(centred_dwconv(z_qk, conv_w))

    def heads(t):
        return t.astype(jnp.float32).reshape(B, S, MLSTM_HEADS, MLSTM_HEAD_DIM).transpose(0, 2, 1, 3)

    q = heads(qk[..., :MLSTM_WIDTH])
    k = heads(qk[..., MLSTM_WIDTH:]) * (MLSTM_HEAD_DIM ** -0.5)
    v = heads(z_v)
    g = (z_g + gate_bias).astype(jnp.float32).transpose(0, 2, 1)
    i_f, f_f, i_b, f_b = jnp.split(g, 4, axis=1)
    h_fwd = mlstm_direction(q, k, v, i_f, f_f)

    def flip(t):
        return jnp.flip(t, axis=2)

    h_bwd = flip(mlstm_direction(flip(q), flip(k), flip(v), flip(i_b), flip(f_b)))
    h = (h_fwd + h_bwd).transpose(0, 2, 1, 3)
    h = rms_norm(h, out_norm.reshape(MLSTM_HEADS, MLSTM_HEAD_DIM))
    out = jax.nn.sigmoid(z_o.astype(jnp.float32)) * h.reshape(B, S, MLSTM_WIDTH)
    return out.astype(z_o.dtype)


def cross_attention(h, mem_n, w_xq, w_xkv, w_xo):
    B, S, _ = h.shape
    M = mem_n.shape[1]
    q = (h @ w_xq).reshape(B, S, X_HEADS, X_HEAD_DIM)
    kv = (mem_n @ w_xkv).reshape(B, M, 2, X_HEADS, X_HEAD_DIM)
    k, v = kv[:, :, 0], kv[:, :, 1]
    s = jnp.einsum('bqhd,bmhd->bhqm', q, k).astype(jnp.float32) * (X_HEAD_DIM ** -0.5)
    p = jax.nn.softmax(s, axis=-1).astype(v.dtype)
    o = jnp.einsum('bhqm,bmhd->bqhd', p, v).reshape(B, S, X_HEADS * X_HEAD_DIM)
    return o @ w_xo


def swiglu(h, w_gate_up, w_down):
    g, u = jnp.split(h @ w_gate_up, 2, axis=-1)
    return (jax.nn.silu(g) * u) @ w_down


def setup_inputs(seed: int = 0) -> dict:
    key = jax.random.key(seed)
    ks = jax.random.split(key, 26)
    f32 = jnp.float32

    def w(k, shape, fan_in):
        return jax.random.normal(k, shape, f32) * (fan_in ** -0.5)

    def gain(k, shape):
        return 1.0 + 0.02 * jax.random.normal(k, shape, f32)

    x = jax.random.normal(ks[0], (BATCH, SEQ, D_MODEL), f32)
    mem = jax.random.normal(ks[1], (BATCH, N_MEM, D_MODEL), f32)
    offsets = jax.random.randint(ks[2], (BATCH, 1), 0, 2048, dtype=jnp.int32)
    positions = (offsets + jnp.arange(SEQ, dtype=jnp.int32)[None, :]).astype(jnp.int32)
    i_bias_f = 0.1 * jax.random.normal(ks[3], (DEPTH, MLSTM_HEADS), f32)
    i_bias_b = 0.1 * jax.random.normal(ks[4], (DEPTH, MLSTM_HEADS), f32)
    f_base = jnp.linspace(3.0, 6.0, MLSTM_HEADS, dtype=f32)[None, :]
    f_bias_f = f_base + 0.01 * jax.random.normal(ks[5], (DEPTH, MLSTM_HEADS), f32)
    f_bias_b = f_base + 0.01 * jax.random.normal(ks[6], (DEPTH, MLSTM_HEADS), f32)
    mlstm_gate_bias = jnp.concatenate([i_bias_f, f_bias_f, i_bias_b, f_bias_b], axis=-1)
    return {
        'x': x,
        'mem': mem,
        'positions': positions,
        'attn_norm': gain(ks[7], (DEPTH, D_MODEL)),
        'w_in': w(ks[8], (DEPTH, D_MODEL, IN_WIDTH), D_MODEL),
        'q_norm': gain(ks[9], (DEPTH, Q_LORA)),
        'w_uq': w(ks[10], (DEPTH, Q_LORA, MLA_HEADS * (NOPE_DIM + ROPE_DIM)), Q_LORA),
        'kv_norm': gain(ks[11], (DEPTH, KV_LORA)),
        'w_ukv': w(ks[12], (DEPTH, KV_LORA, MLA_HEADS * (NOPE_DIM + V_DIM)), KV_LORA),
        'mlstm_conv': w(ks[13], (DEPTH, MLSTM_CONV, 2 * MLSTM_WIDTH), MLSTM_CONV),
        'mlstm_gate_bias': mlstm_gate_bias,
        'mlstm_norm': gain(ks[14], (DEPTH, MLSTM_WIDTH)),
        'w_out': w(ks[15], (DEPTH, MIX_WIDTH, D_MODEL), MIX_WIDTH),
        'xattn_norm': gain(ks[16], (DEPTH, D_MODEL)),
        'mem_norm': gain(ks[17], (DEPTH, D_MODEL)),
        'w_xq': w(ks[18], (DEPTH, D_MODEL, X_HEADS * X_HEAD_DIM), D_MODEL),
        'w_xkv': w(ks[19], (DEPTH, D_MODEL, 2 * X_HEADS * X_HEAD_DIM), D_MODEL),
        'w_xo': w(ks[20], (DEPTH, X_HEADS * X_HEAD_DIM, D_MODEL), X_HEADS * X_HEAD_DIM),
        'ffn_norm': gain(ks[21], (DEPTH, D_MODEL)),
        'w_gate_up': w(ks[22], (DEPTH, D_MODEL, 2 * D_FF), D_MODEL),
        'w_down': w(ks[23], (DEPTH, D_FF, D_MODEL), D_FF),
        'final_norm': gain(ks[24], (D_MODEL,)),
    }


def reference(x, mem, positions, attn_norm, w_in, q_norm, w_uq, kv_norm, w_ukv,
              mlstm_conv, mlstm_gate_bias, mlstm_norm, w_out, xattn_norm, mem_norm,
              w_xq, w_xkv, w_xo, ffn_norm, w_gate_up, w_down, final_norm):
    cos, sin = rope_tables(positions)
    split_idx = [int(c) for c in np.cumsum(IN_SPLITS)[:-1]]
    for l in range(DEPTH):
        h = rms_norm(x, attn_norm[l])
        z_cq, z_ckv, z_kr, z_qk, z_v, z_o, z_g = jnp.split(h @ w_in[l], split_idx, axis=-1)
        y_mla = mla_attention(z_cq, z_ckv, z_kr, q_norm[l], w_uq[l], kv_norm[l], w_ukv[l], cos, sin)
        y_mlstm = mlstm_mixer(z_qk, z_v, z_o, z_g, mlstm_conv[l], mlstm_gate_bias[l], mlstm_norm[l])
        x = x + jnp.concatenate([y_mla, y_mlstm], axis=-1) @ w_out[l]
        x = x + cross_attention(rms_norm(x, xattn_norm[l]), rms_norm(mem, mem_norm[l]),
                                w_xq[l], w_xkv[l], w_xo[l])
        x = x + swiglu(rms_norm(x, ffn_norm[l]), w_gate_up[l], w_down[l])
    return rms_norm(x, final_norm)
```

```python
import functools

import numpy as np
import jax
import jax.numpy as jnp
from jax import lax
from jax.experimental import pallas as pl
from jax.experimental.pallas import tpu as pltpu

F32 = jnp.float32
BF16 = jnp.bfloat16

EPS = 1e-6
MLA_HEADS = 8
Q_LORA = 256
KV_LORA = 128
NOPE_DIM = 64
ROPE_DIM = 32
V_DIM = 64
ROPE_THETA = 10000.0
MLSTM_HEADS = 4
MLSTM_HEAD_DIM = 128
MLSTM_WIDTH = MLSTM_HEADS * MLSTM_HEAD_DIM
MLSTM_CHUNK = 128
X_HEADS = 4
LANES = 128
HALF_ROPE = ROPE_DIM // 2

VMEM_LIMIT = 56 * 1024 * 1024


def _cparams(sem):
    return pltpu.CompilerParams(dimension_semantics=sem, vmem_limit_bytes=VMEM_LIMIT)


def _rms(x, g):
    ms = jnp.mean(x * x, axis=-1, keepdims=True)
    return x * lax.rsqrt(ms + EPS) * g


def _dot(a, b):
    return jnp.dot(a, b, preferred_element_type=F32)


def _dot_nt(a, b):
    return lax.dot_general(a, b, (((1,), (1,)), ((), ())), preferred_element_type=F32)


def _dot_tn(a, b):
    return lax.dot_general(a, b, (((0,), (0,)), ((), ())), preferred_element_type=F32)


def _const_spec(shape):
    return pl.BlockSpec(shape, lambda *_: (0,) * len(shape))


def _mem_kv_kernel(mem_ref, g_ref, w_ref, k_ref, v_ref):
    d = mem_ref.shape[-1]
    mn = _rms(mem_ref[0], g_ref[...]).astype(BF16)
    kv = _dot(mn, w_ref[...])
    k_ref[0] = kv[:, :d].astype(BF16)
    v_ref[0] = kv[:, d:].astype(BF16)


def _mem_kv(mem, mem_norm, w_xkv):
    B, M, D = mem.shape
    return pl.pallas_call(
        _mem_kv_kernel,
        grid=(B,),
        in_specs=[pl.BlockSpec((1, M, D), lambda b: (b, 0, 0)),
                  _const_spec((1, D)), _const_spec((D, 2 * D))],
        out_specs=[pl.BlockSpec((1, M, D), lambda b: (b, 0, 0))] * 2,
        out_shape=[jax.ShapeDtypeStruct((B, M, D), BF16)] * 2,
        compiler_params=_cparams(("arbitrary",)),
        name="mem_kv",
    )(mem, mem_norm, w_xkv)


def _inproj_kernel(x_ref, ct_ref, st_ref, an_ref, wa_ref, wb_ref, qn_ref, wq_ref, kvn_ref,
                   wkv_ref, gb_ref, vone_ref,
                   q_ref, k_ref, v_ref, zqk_ref, zv_ref, zo_ref, zg_ref, *, q_scale):
    hw = MLA_HEADS * LANES
    h = _rms(x_ref[...], an_ref[...]).astype(BF16)
    za = _dot(h, wa_ref[...])
    cq = za[:, :Q_LORA]
    ckv = za[:, Q_LORA:Q_LORA + KV_LORA]
    kra = za[:, Q_LORA + KV_LORA:Q_LORA + KV_LORA + LANES]
    krb = za[:, Q_LORA + KV_LORA + LANES:]
    ct = ct_ref[...]
    st = st_ref[...]
    cqn = _rms(cq, qn_ref[...]).astype(BF16)
    ckvn = _rms(ckv, kvn_ref[...]).astype(BF16)
    qq = _dot(cqn, wq_ref[...])
    q = qq[:, :hw] * jnp.tile(ct, (1, MLA_HEADS)) + qq[:, hw:] * jnp.tile(st, (1, MLA_HEADS))
    q_ref[...] = (q * q_scale).astype(BF16)
    kv = _dot(ckvn, wkv_ref[...])
    krot = kra * ct + krb * st
    k_ref[...] = (kv[:, :hw] + jnp.tile(krot, (1, MLA_HEADS))).astype(BF16)
    v_ref[...] = (kv[:, hw:] + vone_ref[...]).astype(BF16)
    zb = _dot(h, wb_ref[...])
    w = MLSTM_WIDTH
    zqk_ref[...] = zb[:, :2 * w].astype(BF16)
    zv_ref[...] = zb[:, 2 * w:3 * w].astype(BF16)
    zo_ref[...] = zb[:, 3 * w:4 * w].astype(BF16)
    zg_ref[...] = zb[:, 4 * w:] + gb_ref[...]


def _in_proj(x2, ctab, stab, attn_norm, wa, wb, q_norm, wq, kv_norm, wkv, gbias, vone, tm):
    T, D = x2.shape
    hw = MLA_HEADS * LANES
    w = MLSTM_WIDTH
    row = lambda n: pl.BlockSpec((tm, n), lambda i: (i, 0))
    outs = [(hw, BF16), (hw, BF16), (hw, BF16), (2 * w, BF16), (w, BF16), (w, BF16), (2 * LANES, F32)]
    q_scale = float((NOPE_DIM + ROPE_DIM) ** -0.5)
    return pl.pallas_call(
        functools.partial(_inproj_kernel, q_scale=q_scale),
        grid=(T // tm,),
        in_specs=[row(D), row(LANES), row(LANES), _const_spec(attn_norm.shape), _const_spec(wa.shape),
                  _const_spec(wb.shape), _const_spec(q_norm.shape), _const_spec(wq.shape),
                  _const_spec(kv_norm.shape), _const_spec(wkv.shape), _const_spec(gbias.shape),
                  _const_spec(vone.shape)],
        out_specs=[row(n) for n, _ in outs],
        out_shape=[jax.ShapeDtypeStruct((T, n), dt) for n, dt in outs],
        compiler_params=_cparams(("arbitrary",)),
        name="in_proj",
    )(x2, ctab, stab, attn_norm, wa, wb, q_norm, wq, kv_norm, wkv, gbias, vone)


def _mla_attn_kernel(q_ref, k_ref, v_ref, o_ref, *, tk):
    q = q_ref[0]
    tq = q.shape[0]
    nk = k_ref.shape[1] // tk

    def body(j, carry):
        m, acc = carry
        off = pl.multiple_of(j * tk, tk)
        kj = k_ref[0, pl.ds(off, tk), :]
        vj = v_ref[0, pl.ds(off, tk), :]
        s = _dot_nt(q, kj)
        m_new = jnp.maximum(m, jnp.max(s, axis=-1, keepdims=True))
        alpha = jnp.exp(m - m_new)
        p = jnp.exp(s - m_new)
        acc = alpha * acc + _dot(p.astype(BF16), vj)
        return m_new, acc

    m0 = jnp.full((tq, 1), -jnp.inf, F32)
    acc0 = jnp.zeros((tq, LANES), F32)
    _, acc = lax.fori_loop(0, nk, body, (m0, acc0))
    o_ref[0] = (acc * (1.0 / acc[:, V_DIM:V_DIM + 1])).astype(BF16)


def _mla_attention(q, k, v, tq, tk):
    B, S, _ = q.shape
    return pl.pallas_call(
        functools.partial(_mla_attn_kernel, tk=tk),
        grid=(B, MLA_HEADS, S // tq),
        in_specs=[pl.BlockSpec((1, tq, LANES), lambda b, h, i: (b, i, h)),
                  pl.BlockSpec((1, S, LANES), lambda b, h, i: (b, 0, h)),
                  pl.BlockSpec((1, S, LANES), lambda b, h, i: (b, 0, h))],
        out_specs=pl.BlockSpec((1, tq, LANES), lambda b, h, i: (b, i, h)),
        out_shape=jax.ShapeDtypeStruct(q.shape, BF16),
        compiler_params=_cparams(("arbitrary", "arbitrary", "arbitrary")),
        name="mla_attn",
    )(q, k, v)


def _log_sigmoid(x):
    return jnp.minimum(x, 0.0) - jnp.log(1.0 + jnp.exp(-jnp.abs(x)))


def _cumsum_matmul(tri, x):
    hi = x.astype(BF16)
    r1 = x - hi.astype(F32)
    mid = r1.astype(BF16)
    lo = (r1 - mid.astype(F32)).astype(BF16)
    return _dot(tri, hi) + _dot(tri, mid) + _dot(tri, lo)


def _conv_silu(z_ref, zp_ref, zn_ref, has_prev, has_next, w, pad_ref):
    L = z_ref.shape[1]
    hp = zp_ref.shape[1]
    prev = zp_ref[0].astype(F32)[hp - 1:hp, :]
    nxt = zn_ref[0].astype(F32)[0:1, :]
    pad_ref[8:8 + L, :] = z_ref[0].astype(F32)
    pad_ref[7:8, :] = jnp.where(has_prev, prev, 0.0)
    pad_ref[8 + L:9 + L, :] = jnp.where(has_next, nxt, 0.0)
    y = pad_ref[7:7 + L, :] * w[0:1, :] + pad_ref[8:8 + L, :] * w[1:2, :]
    y = y + pad_ref[9:9 + L, :] * w[2:3, :]
    return y * jax.nn.sigmoid(y)


def _mlstm_direction(qk, v_ref, gi, gf, col0, rev, out_ref, c_ref, n_ref, m_ref):
    L = MLSTM_CHUNK
    dh = MLSTM_HEAD_DIM
    w = MLSTM_WIDTH
    rows = lax.broadcasted_iota(jnp.int32, (L, L), 0)
    cols = lax.broadcasted_iota(jnp.int32, (L, L), 1)
    mask = (cols >= rows) if rev else (cols <= rows)
    tri = jnp.where(mask, 1.0, 0.0).astype(BF16)
    logf = _log_sigmoid(gf)
    b = _cumsum_matmul(tri, logf)
    blast = b[0:1, :] if rev else b[L - 1:L, :]
    m_row = m_ref[...]
    w_end = blast - b + gi
    m_loc = jnp.max(w_end, axis=0, keepdims=True)
    e_end = jnp.exp(w_end - m_loc)
    u_rows = (gi - b).T
    log_inter_all = b + m_row
    m_new = jnp.maximum(blast + m_row, m_loc)
    a_row = jnp.exp(blast + m_row - m_new)
    c_row = jnp.exp(m_loc - m_new)
    m_ref[...] = jnp.where(
        (lax.broadcasted_iota(jnp.int32, m_row.shape, 1) >= col0)
        & (lax.broadcasted_iota(jnp.int32, m_row.shape, 1) < col0 + MLSTM_HEADS), m_new, m_row)
    k_scale = float(dh ** -0.5)
    for h in range(MLSTM_HEADS):
        j = col0 + h
        q_f = qk[:, h * dh:(h + 1) * dh]
        k_f = qk[:, w + h * dh:w + (h + 1) * dh] * k_scale
        v_f = v_ref[0, :, h * dh:(h + 1) * dh].astype(F32)
        q_b = q_f.astype(BF16)
        k_b = k_f.astype(BF16)
        d = b[:, j:j + 1] + u_rows[j:j + 1, :]
        d = jnp.where(mask, d, -jnp.inf)
        log_inter = log_inter_all[:, j:j + 1]
        m_t = jnp.maximum(log_inter, jnp.max(d, axis=-1, keepdims=True))
        p = jnp.exp(d - m_t)
        a_t = jnp.exp(log_inter - m_t)
        s = _dot_nt(q_b, k_b) * p
        c_prev = c_ref[j]
        n_prev = n_ref[j]
        num = _dot(s.astype(BF16), v_f.astype(BF16)) + a_t * _dot_nt(q_b, c_prev.astype(BF16))
        den = jnp.sum(s, axis=-1, keepdims=True) + a_t * jnp.sum(q_f * n_prev, axis=-1, keepdims=True)
        out_ref[0, :, h * dh:(h + 1) * dh] = num / jnp.maximum(jnp.abs(den), jnp.exp(-m_t))
        e_col = e_end[:, j:j + 1]
        c_loc = _dot_tn((v_f * e_col).astype(BF16), k_b)
        n_loc = jnp.sum(k_f * e_col, axis=0, keepdims=True)
        a_j = a_row[:, j:j + 1]
        c_j = c_row[:, j:j + 1]
        c_ref[j] = a_j * c_prev + c_j * c_loc
        n_ref[j] = a_j * n_prev + c_j * n_loc


def _mlstm_kernel(zf_ref, zfp_ref, zfn_ref, zb_ref, zbp_ref, zbn_ref, vf_ref, vb_ref, gf_ref, gb_ref,
                  cw_ref, hf_ref, hb_ref, c_ref, n_ref, m_ref, padf_ref, padb_ref):
    c = pl.program_id(1)
    nc = pl.num_programs(1)

    @pl.when(c == 0)
    def _():
        c_ref[...] = jnp.zeros_like(c_ref)
        n_ref[...] = jnp.zeros_like(n_ref)
        m_ref[...] = jnp.zeros_like(m_ref)

    cw = cw_ref[...]
    qk_f = _conv_silu(zf_ref, zfp_ref, zfn_ref, c > 0, c < nc - 1, cw, padf_ref)
    qk_b = _conv_silu(zb_ref, zbp_ref, zbn_ref, c < nc - 1, c > 0, cw, padb_ref)
    g_f = gf_ref[0]
    g_b = gb_ref[0]
    _mlstm_direction(qk_f, vf_ref, g_f[:, :LANES], g_f[:, LANES:], 0, False, hf_ref, c_ref, n_ref, m_ref)
    _mlstm_direction(qk_b, vb_ref, g_b[:, :LANES], g_b[:, LANES:], MLSTM_HEADS, True, hb_ref,
                     c_ref, n_ref, m_ref)


def _mlstm(zqk, zv, zg, conv_w):
    B, S, _ = zqk.shape
    L = MLSTM_CHUNK
    nc = S // L
    w = MLSTM_WIDTH
    dh = MLSTM_HEAD_DIM
    hr = 16
    per = L // hr
    nhb = S // hr
    fwd = lambda b, c: (b, c, 0)
    bwd = lambda b, c: (b, nc - 1 - c, 0)
    f_prev = lambda b, c: (b, jnp.maximum(c * per - 1, 0), 0)
    f_next = lambda b, c: (b, jnp.minimum((c + 1) * per, nhb - 1), 0)
    b_prev = lambda b, c: (b, jnp.maximum((nc - 1 - c) * per - 1, 0), 0)
    b_next = lambda b, c: (b, jnp.minimum((nc - c) * per, nhb - 1), 0)
    return pl.pallas_call(
        _mlstm_kernel,
        grid=(B, nc),
        in_specs=[pl.BlockSpec((1, L, 2 * w), fwd), pl.BlockSpec((1, hr, 2 * w), f_prev),
                  pl.BlockSpec((1, hr, 2 * w), f_next),
                  pl.BlockSpec((1, L, 2 * w), bwd), pl.BlockSpec((1, hr, 2 * w), b_prev),
                  pl.BlockSpec((1, hr, 2 * w), b_next),
                  pl.BlockSpec((1, L, w), fwd), pl.BlockSpec((1, L, w), bwd),
                  pl.BlockSpec((1, L, 2 * LANES), fwd), pl.BlockSpec((1, L, 2 * LANES), bwd),
                  _const_spec(conv_w.shape)],
        out_specs=[pl.BlockSpec((1, L, w), fwd), pl.BlockSpec((1, L, w), bwd)],
        out_shape=[jax.ShapeDtypeStruct((B, S, w), F32)] * 2,
        scratch_shapes=[pltpu.VMEM((2 * MLSTM_HEADS, dh, dh), F32),
                        pltpu.VMEM((2 * MLSTM_HEADS, 1, dh), F32),
                        pltpu.VMEM((1, LANES), F32),
                        pltpu.VMEM((L + 16, 2 * w), F32),
                        pltpu.VMEM((L + 16, 2 * w), F32)],
        compiler_params=_cparams(("arbitrary", "arbitrary")),
        name="mlstm",
    )(zqk, zqk, zqk, zqk, zqk, zqk, zv, zv, zg, zg, conv_w)


def _mix_xattn_kernel(x_ref, ya_ref, hf_ref, hb_ref, zo_ref, mn_ref, wom_ref, wol_ref, xn_ref, wxq_ref,
                      kx_ref, vx_ref, wxo_ref, o_ref, *, x_scale):
    dh = MLSTM_HEAD_DIM
    hs = hf_ref[...] + hb_ref[...]
    parts = []
    for h in range(MLSTM_HEADS):
        sl = slice(h * dh, (h + 1) * dh)
        parts.append(_rms(hs[:, sl], mn_ref[:, sl]))
    hn = jnp.concatenate(parts, axis=-1)
    yl = (jax.nn.sigmoid(zo_ref[...].astype(F32)) * hn).astype(BF16)
    x1 = x_ref[...] + _dot(ya_ref[...], wom_ref[...]) + _dot(yl, wol_ref[...])
    hq = _rms(x1, xn_ref[...]).astype(BF16)
    q = (_dot(hq, wxq_ref[...]) * x_scale).astype(BF16)
    xd = q.shape[-1] // X_HEADS
    outs = []
    for h in range(X_HEADS):
        sl = slice(h * xd, (h + 1) * xd)
        s = _dot_nt(q[:, sl], kx_ref[0, :, sl])
        p = jnp.exp(s - jnp.max(s, axis=-1, keepdims=True))
        l = jnp.sum(p, axis=-1, keepdims=True)
        outs.append((_dot(p.astype(BF16), vx_ref[0, :, sl]) * (1.0 / l)).astype(BF16))
    o = jnp.concatenate(outs, axis=-1)
    o_ref[...] = x1 + _dot(o, wxo_ref[...])


def _mix_xattn(x2, ya, hf, hb, zo, mlstm_norm, wom, wol, xattn_norm, wxq, kx, vx, wxo, tm, S):
    T, D = x2.shape
    M = kx.shape[1]
    per_b = S // tm
    row = lambda n: pl.BlockSpec((tm, n), lambda i: (i, 0))
    mem_spec = pl.BlockSpec((1, M, D), lambda i: (i // per_b, 0, 0))
    x_scale = float((D // X_HEADS) ** -0.5)
    return pl.pallas_call(
        functools.partial(_mix_xattn_kernel, x_scale=x_scale),
        grid=(T // tm,),
        in_specs=[row(D), row(ya.shape[1]), row(hf.shape[1]), row(hb.shape[1]), row(zo.shape[1]),
                  _const_spec(mlstm_norm.shape), _const_spec(wom.shape), _const_spec(wol.shape),
                  _const_spec(xattn_norm.shape), _const_spec(wxq.shape), mem_spec, mem_spec,
                  _const_spec(wxo.shape)],
        out_specs=row(D),
        out_shape=jax.ShapeDtypeStruct((T, D), F32),
        compiler_params=_cparams(("arbitrary",)),
        name="mix_xattn",
    )(x2, ya, hf, hb, zo, mlstm_norm, wom, wol, xattn_norm, wxq, kx, vx, wxo)


def _ffn_kernel(x_ref, fn_ref, wg_ref, wu_ref, wd_ref, on_ref, o_ref, *, fc):
    x = x_ref[...]
    hn = _rms(x, fn_ref[...]).astype(BF16)
    dff = wg_ref.shape[1]
    acc = x
    for j in range(dff // fc):
        sl = slice(j * fc, (j + 1) * fc)
        g = _dot(hn, wg_ref[:, sl])
        u = _dot(hn, wu_ref[:, sl])
        acc = acc + _dot((g * jax.nn.sigmoid(g) * u).astype(BF16), wd_ref[sl, :])
    o_ref[...] = _rms(acc, on_ref[...])


def _ffn(x2, ffn_norm, wg, wu, wd, final_norm, tm, fc):
    T, D = x2.shape
    row = pl.BlockSpec((tm, D), lambda i: (i, 0))
    return pl.pallas_call(
        functools.partial(_ffn_kernel, fc=fc),
        grid=(T // tm,),
        in_specs=[row, _const_spec(ffn_norm.shape), _const_spec(wg.shape), _const_spec(wu.shape),
                  _const_spec(wd.shape), _const_spec(final_norm.shape)],
        out_specs=row,
        out_shape=jax.ShapeDtypeStruct((T, D), F32),
        compiler_params=_cparams(("arbitrary",)),
        name="ffn",
    )(x2, ffn_norm, wg, wu, wd, final_norm)


def _head_blocks(wmat, per_head, take, put):
    rows = wmat.shape[0]
    w3 = wmat.reshape(rows, MLA_HEADS, per_head)[:, :, take[0]:take[1]]
    out = jnp.zeros((rows, MLA_HEADS, LANES), wmat.dtype)
    out = out.at[:, :, put:put + (take[1] - take[0])].set(w3)
    return out.reshape(rows, MLA_HEADS * LANES)


def _lane_block(wmat, put):
    rows, n = wmat.shape
    return jnp.zeros((rows, LANES), wmat.dtype).at[:, put:put + n].set(wmat)


def kernel(x, mem, positions, attn_norm, w_in, q_norm, w_uq, kv_norm, w_ukv, mlstm_conv, mlstm_gate_bias,
           mlstm_norm, w_out, xattn_norm, mem_norm, w_xq, w_xkv, w_xo, ffn_norm, w_gate_up, w_down,
           final_norm):
    B, S, D = x.shape
    T = B * S
    assert w_in.shape[0] == 1, "single-layer problem: the final norm is fused into the SwiGLU kernel"
    l = 0
    tm = 512
    x2 = x.reshape(T, D)

    inv = ROPE_THETA ** (-jnp.arange(0, ROPE_DIM, 2, dtype=F32) / ROPE_DIM)
    ang = positions.astype(F32).reshape(T, 1) * inv
    cos, sin = jnp.cos(ang), jnp.sin(ang)
    pad = jnp.zeros((T, LANES - NOPE_DIM - ROPE_DIM), F32)
    ctab = jnp.concatenate([jnp.ones((T, NOPE_DIM), F32), cos, cos, pad], axis=-1)
    stab = jnp.concatenate([jnp.zeros((T, NOPE_DIM), F32), -sin, sin, pad], axis=-1)
    vone = jnp.tile(jnp.zeros((1, LANES), F32).at[0, V_DIM].set(1.0), (1, MLA_HEADS))

    o_cq, o_ckv, o_kr = 0, Q_LORA, Q_LORA + KV_LORA
    o_qk = o_kr + ROPE_DIM
    o_v = o_qk + 2 * MLSTM_WIDTH
    o_o = o_v + MLSTM_WIDTH
    o_g = o_o + MLSTM_WIDTH
    nh = MLSTM_HEADS
    qd = NOPE_DIM + ROPE_DIM
    wi = w_in[l]
    kr1 = wi[:, o_kr:o_kr + HALF_ROPE]
    kr2 = wi[:, o_kr + HALF_ROPE:o_qk]
    kr_a = _lane_block(jnp.concatenate([kr1, kr2], axis=1), NOPE_DIM)
    kr_b = _lane_block(jnp.concatenate([kr2, kr1], axis=1), NOPE_DIM)
    wa = jnp.concatenate([wi[:, o_cq:o_kr], kr_a, kr_b], axis=1).astype(BF16)
    wg = wi[:, o_g:]
    wgi = _lane_block(jnp.concatenate([wg[:, 0:nh], wg[:, 2 * nh:3 * nh]], axis=1), 0)
    wgf = _lane_block(jnp.concatenate([wg[:, nh:2 * nh], wg[:, 3 * nh:4 * nh]], axis=1), 0)
    wb = jnp.concatenate([wi[:, o_qk:o_g], wgi, wgf], axis=1).astype(BF16)
    gb = mlstm_gate_bias[l][None, :]
    gbias = jnp.concatenate(
        [_lane_block(jnp.concatenate([gb[:, 0:nh], gb[:, 2 * nh:3 * nh]], axis=1), 0),
         _lane_block(jnp.concatenate([gb[:, nh:2 * nh], gb[:, 3 * nh:4 * nh]], axis=1), 0)], axis=1)
    wq = w_uq[l]
    wq_a = (_head_blocks(wq, qd, (0, NOPE_DIM), 0)
            + _head_blocks(wq, qd, (NOPE_DIM, qd), NOPE_DIM))
    wq_b = (_head_blocks(wq, qd, (NOPE_DIM + HALF_ROPE, qd), NOPE_DIM)
            + _head_blocks(wq, qd, (NOPE_DIM, NOPE_DIM + HALF_ROPE), NOPE_DIM + HALF_ROPE))
    wqq = jnp.concatenate([wq_a, wq_b], axis=1).astype(BF16)
    wkv = w_ukv[l]
    kvd = NOPE_DIM + V_DIM
    wkv2 = jnp.concatenate([_head_blocks(wkv, kvd, (0, NOPE_DIM), 0),
                            _head_blocks(wkv, kvd, (NOPE_DIM, kvd), 0)], axis=1).astype(BF16)
    wo = w_out[l]
    wom = jnp.zeros((MLA_HEADS, LANES, D), F32).at[:, :V_DIM, :].set(
        wo[:MLA_HEADS * V_DIM].reshape(MLA_HEADS, V_DIM, D)).reshape(MLA_HEADS * LANES, D).astype(BF16)
    wol = wo[MLA_HEADS * V_DIM:].astype(BF16)
    dff = w_down.shape[1]

    kx, vx = _mem_kv(mem, mem_norm[l][None, :], w_xkv[l].astype(BF16))
    q, k, v, zqk, zv, zo, zg = _in_proj(
        x2, ctab, stab, attn_norm[l][None, :], wa, wb, q_norm[l][None, :], wqq, kv_norm[l][None, :],
        wkv2, gbias, vone, tm)
    hw = MLA_HEADS * LANES
    ya = _mla_attention(q.reshape(B, S, hw), k.reshape(B, S, hw), v.reshape(B, S, hw), 512, 512)
    hf, hb = _mlstm(zqk.reshape(B, S, -1), zv.reshape(B, S, -1), zg.reshape(B, S, -1), mlstm_conv[l])
    x2 = _mix_xattn(x2, ya.reshape(T, hw), hf.reshape(T, -1), hb.reshape(T, -1), zo,
                    mlstm_norm[l][None, :], wom, wol, xattn_norm[l][None, :], w_xq[l].astype(BF16),
                    kx, vx, w_xo[l].astype(BF16), tm, S)
    wgu = w_gate_up[l].astype(BF16)
    y = _ffn(x2, ffn_norm[l][None, :], wgu[:, :dff], wgu[:, dff:], w_down[l].astype(BF16),
             final_norm[None, :], tm, 256)
    return y.reshape(B, S, D)
```

```python
import functools

import numpy as np
import jax
import jax.numpy as jnp
from jax import lax
from jax.experimental import pallas as pl
from jax.experimental.pallas import tpu as pltpu

F32 = jnp.float32
BF16 = jnp.bfloat16

EPS = 1e-6
MLA_HEADS = 8
Q_LORA = 256
KV_LORA = 128
NOPE_DIM = 64
ROPE_DIM = 32
V_DIM = 64
ROPE_THETA = 10000.0
MLSTM_HEADS = 4
MLSTM_HEAD_DIM = 128
MLSTM_WIDTH = MLSTM_HEADS * MLSTM_HEAD_DIM
MLSTM_CHUNK = 128
X_HEADS = 4
LANES = 128
HALF_ROPE = ROPE_DIM // 2

VMEM_LIMIT = 56 * 1024 * 1024


def _cparams(sem):
    return pltpu.CompilerParams(dimension_semantics=sem, vmem_limit_bytes=VMEM_LIMIT)


def _rms(x, g):
    ms = jnp.mean(x * x, axis=-1, keepdims=True)
    return x * lax.rsqrt(ms + EPS) * g


def _dot(a, b):
    return jnp.dot(a, b, preferred_element_type=F32)


def _dot_nt(a, b):
    return lax.dot_general(a, b, (((1,), (1,)), ((), ())), preferred_element_type=F32)


def _dot_tn(a, b):
    return lax.dot_general(a, b, (((0,), (0,)), ((), ())), preferred_element_type=F32)


def _const_spec(shape):
    return pl.BlockSpec(shape, lambda *_: (0,) * len(shape))


def _mem_kv_kernel(mem_ref, g_ref, w_ref, k_ref, v_ref):
    d = mem_ref.shape[-1]
    mn = _rms(mem_ref[0], g_ref[...]).astype(BF16)
    kv = _dot(mn, w_ref[...])
    k_ref[0] = kv[:, :d].astype(BF16)
    v_ref[0] = kv[:, d:].astype(BF16)


def _mem_kv(mem, mem_norm, w_xkv):
    B, M, D = mem.shape
    return pl.pallas_call(
        _mem_kv_kernel,
        grid=(B,),
        in_specs=[pl.BlockSpec((1, M, D), lambda b: (b, 0, 0)),
                  _const_spec((1, D)), _const_spec((D, 2 * D))],
        out_specs=[pl.BlockSpec((1, M, D), lambda b: (b, 0, 0))] * 2,
        out_shape=[jax.ShapeDtypeStruct((B, M, D), BF16)] * 2,
        compiler_params=_cparams(("arbitrary",)),
        name="mem_kv",
    )(mem, mem_norm, w_xkv)


def _inproj_kernel(x_ref, ct_ref, st_ref, an_ref, wa_ref, wb_ref, qn_ref, wq_ref, kvn_ref,
                   wkv_ref, gb_ref, vone_ref,
                   q_ref, k_ref, v_ref, zqk_ref, zv_ref, zo_ref, zg_ref, *, q_scale):
    hw = MLA_HEADS * LANES
    h = _rms(x_ref[...], an_ref[...]).astype(BF16)
    za = _dot(h, wa_ref[...])
    cq = za[:, :Q_LORA]
    ckv = za[:, Q_LORA:Q_LORA + KV_LORA]
    kra = za[:, Q_LORA + KV_LORA:Q_LORA + KV_LORA + LANES]
    krb = za[:, Q_LORA + KV_LORA + LANES:]
    ct = ct_ref[...]
    st = st_ref[...]
    cqn = _rms(cq, qn_ref[...]).astype(BF16)
    ckvn = _rms(ckv, kvn_ref[...]).astype(BF16)
    qq = _dot(cqn, wq_ref[...])
    q = qq[:, :hw] * jnp.tile(ct, (1, MLA_HEADS)) + qq[:, hw:] * jnp.tile(st, (1, MLA_HEADS))
    q_ref[...] = (q * q_scale).astype(BF16)
    kv = _dot(ckvn, wkv_ref[...])
    krot = kra * ct + krb * st
    k_ref[...] = (kv[:, :hw] + jnp.tile(krot, (1, MLA_HEADS))).astype(BF16)
    v_ref[...] = (kv[:, hw:] + vone_ref[...]).astype(BF16)
    zb = _dot(h, wb_ref[...])
    w = MLSTM_WIDTH
    zqk_ref[...] = zb[:, :2 * w].astype(BF16)
    zv_ref[...] = zb[:, 2 * w:3 * w].astype(BF16)
    zo_ref[...] = zb[:, 3 * w:4 * w].astype(BF16)
    zg_ref[...] = zb[:, 4 * w:] + gb_ref[...]


def _in_proj(x2, ctab, stab, attn_norm, wa, wb, q_norm, wq, kv_norm, wkv, gbias, vone, tm):
    T, D = x2.shape
    hw = MLA_HEADS * LANES
    w = MLSTM_WIDTH
    row = lambda n: pl.BlockSpec((tm, n), lambda i: (i, 0))
    outs = [(hw, BF16), (hw, BF16), (hw, BF16), (2 * w, BF16), (w, BF16), (w, BF16), (2 * LANES, F32)]
    q_scale = float((NOPE_DIM + ROPE_DIM) ** -0.5 * np.log2(np.e))
    return pl.pallas_call(
        functools.partial(_inproj_kernel, q_scale=q_scale),
        grid=(T // tm,),
        in_specs=[row(D), row(LANES), row(LANES), _const_spec(attn_norm.shape), _const_spec(wa.shape),
                  _const_spec(wb.shape), _const_spec(q_norm.shape), _const_spec(wq.shape),
                  _const_spec(kv_norm.shape), _const_spec(wkv.shape), _const_spec(gbias.shape),
                  _const_spec(vone.shape)],
        out_specs=[row(n) for n, _ in outs],
        out_shape=[jax.ShapeDtypeStruct((T, n), dt) for n, dt in outs],
        compiler_params=_cparams(("arbitrary",)),
        name="in_proj",
    )(x2, ctab, stab, attn_norm, wa, wb, q_norm, wq, kv_norm, wkv, gbias, vone)


HEADS_PER_STEP = 2
V_ROWS = 80


def _mla_attn_kernel(q_ref, k_ref, v_ref, o_ref, vt_ref, qt_ref, s_ref, mc_ref, acc_ref, m_ref, *, tk):
    S = k_ref.shape[1]
    tq = q_ref.shape[1]
    nk = S // tk
    assert nk >= 2 and nk % 2 == 0
    heads = range(HEADS_PER_STEP)

    @pl.when(pl.program_id(2) == 0)
    def _():
        for h in heads:
            for c in range(nk):
                blk = v_ref[0, c * tk:(c + 1) * tk, h * LANES:(h + 1) * LANES].astype(F32)
                vt_ref[h, :, c * tk:(c + 1) * tk] = blk.T.astype(BF16)

    for h in heads:
        qt_ref[h] = q_ref[0, :, h * LANES:(h + 1) * LANES].astype(F32).T.astype(BF16)
        m_ref[h] = jnp.full((1, tq), -jnp.inf, F32)
        acc_ref[h] = jnp.zeros((V_ROWS, tq), F32)

    def scores(off, slot):
        for h in heads:
            st = _dot(k_ref[0, pl.ds(off, tk), h * LANES:(h + 1) * LANES], qt_ref[h])
            s_ref[slot, h] = st
            mc_ref[slot, h] = jnp.max(st, axis=0, keepdims=True)

    def consume(off, slot):
        for h in heads:
            m_old = m_ref[h]
            m_new = jnp.maximum(m_old, mc_ref[slot, h])
            alpha = jnp.exp2(m_old - m_new)
            p = jnp.exp2(s_ref[slot, h] - m_new).astype(BF16)
            acc_ref[h] = alpha * acc_ref[h] + _dot(vt_ref[h, 0:V_ROWS, pl.ds(off, tk)], p)
            m_ref[h] = m_new

    def step(c, slot):
        scores(pl.multiple_of((c + 1) * tk, tk), 1 - slot)
        consume(pl.multiple_of(c * tk, tk), slot)

    def body(jj, carry):
        step(2 * jj, 0)
        step(2 * jj + 1, 1)
        return carry

    scores(0, 0)
    lax.fori_loop(0, nk // 2 - 1, body, 0)
    step(nk - 2, 0)
    consume((nk - 1) * tk, 1)
    outs = []
    for h in heads:
        acc = acc_ref[h]
        o_t = acc[0:V_DIM, :] * (1.0 / acc[V_DIM:V_DIM + 1, :])
        outs.append(o_t.T)
    o_ref[0] = jnp.concatenate(outs, axis=-1).astype(BF16)


def _mla_attention(q, k, v, tq, tk):
    B, S, _ = q.shape
    hp = HEADS_PER_STEP
    return pl.pallas_call(
        functools.partial(_mla_attn_kernel, tk=tk),
        grid=(B, MLA_HEADS // hp, S // tq),
        in_specs=[pl.BlockSpec((1, tq, hp * LANES), lambda b, h, i: (b, i, h)),
                  pl.BlockSpec((1, S, hp * LANES), lambda b, h, i: (b, 0, h)),
                  pl.BlockSpec((1, S, hp * LANES), lambda b, h, i: (b, 0, h))],
        out_specs=pl.BlockSpec((1, tq, hp * V_DIM), lambda b, h, i: (b, i, h)),
        out_shape=jax.ShapeDtypeStruct((B, S, MLA_HEADS * V_DIM), BF16),
        scratch_shapes=[pltpu.VMEM((hp, LANES, S), BF16),
                        pltpu.VMEM((hp, LANES, tq), BF16),
                        pltpu.VMEM((2, hp, tk, tq), F32),
                        pltpu.VMEM((2, hp, 1, tq), F32),
                        pltpu.VMEM((hp, V_ROWS, tq), F32),
                        pltpu.VMEM((hp, 1, tq), F32)],
        compiler_params=_cparams(("arbitrary", "arbitrary", "arbitrary")),
        name="mla_attn",
    )(q, k, v)


def _log_sigmoid(x):
    return jnp.minimum(x, 0.0) - jnp.log(1.0 + jnp.exp(-jnp.abs(x)))


def _cumsum_matmul(tri, x):
    hi = x.astype(BF16)
    r1 = x - hi.astype(F32)
    mid = r1.astype(BF16)
    lo = (r1 - mid.astype(F32)).astype(BF16)
    return _dot(tri, hi) + _dot(tri, mid) + _dot(tri, lo)


def _conv_silu(z_ref, zp_ref, zn_ref, has_prev, has_next, w, pad_ref):
    L = z_ref.shape[1]
    hp = zp_ref.shape[1]
    prev = zp_ref[0].astype(F32)[hp - 1:hp, :]
    nxt = zn_ref[0].astype(F32)[0:1, :]
    pad_ref[8:8 + L, :] = z_ref[0].astype(F32)
    pad_ref[7:8, :] = jnp.where(has_prev, prev, 0.0)
    pad_ref[8 + L:9 + L, :] = jnp.where(has_next, nxt, 0.0)
    y = pad_ref[7:7 + L, :] * w[0:1, :] + pad_ref[8:8 + L, :] * w[1:2, :]
    y = y + pad_ref[9:9 + L, :] * w[2:3, :]
    return y * jax.nn.sigmoid(y)


def _mlstm_direction(qk, v_ref, gi, gf, col0, rev, out_ref, c_ref, n_ref, m_ref):
    L = MLSTM_CHUNK
    dh = MLSTM_HEAD_DIM
    w = MLSTM_WIDTH
    rows = lax.broadcasted_iota(jnp.int32, (L, L), 0)
    cols = lax.broadcasted_iota(jnp.int32, (L, L), 1)
    mask = (cols >= rows) if rev else (cols <= rows)
    tri = jnp.where(mask, 1.0, 0.0).astype(BF16)
    logf = _log_sigmoid(gf)
    b = _cumsum_matmul(tri, logf)
    blast = b[0:1, :] if rev else b[L - 1:L, :]
    m_row = m_ref[...]
    w_end = blast - b + gi
    m_loc = jnp.max(w_end, axis=0, keepdims=True)
    e_end = jnp.exp(w_end - m_loc)
    u_rows = (gi - b).T
    log_inter_all = b + m_row
    m_new = jnp.maximum(blast + m_row, m_loc)
    a_row = jnp.exp(blast + m_row - m_new)
    c_row = jnp.exp(m_loc - m_new)
    m_ref[...] = jnp.where(
        (lax.broadcasted_iota(jnp.int32, m_row.shape, 1) >= col0)
        & (lax.broadcasted_iota(jnp.int32, m_row.shape, 1) < col0 + MLSTM_HEADS), m_new, m_row)
    k_scale = float(dh ** -0.5)
    for h in range(MLSTM_HEADS):
        j = col0 + h
        q_f = qk[:, h * dh:(h + 1) * dh]
        k_f = qk[:, w + h * dh:w + (h + 1) * dh] * k_scale
        v_f = v_ref[0, :, h * dh:(h + 1) * dh].astype(F32)
        q_b = q_f.astype(BF16)
        k_b = k_f.astype(BF16)
        d = b[:, j:j + 1] + u_rows[j:j + 1, :]
        d = jnp.where(mask, d, -jnp.inf)
        log_inter = log_inter_all[:, j:j + 1]
        m_t = jnp.maximum(log_inter, jnp.max(d, axis=-1, keepdims=True))
        p = jnp.exp(d - m_t)
        a_t = jnp.exp(log_inter - m_t)
        s = _dot_nt(q_b, k_b) * p
        c_prev = c_ref[j]
        n_prev = n_ref[j]
        num = _dot(s.astype(BF16), v_f.astype(BF16)) + a_t * _dot_nt(q_b, c_prev.astype(BF16))
        den = jnp.sum(s, axis=-1, keepdims=True) + a_t * jnp.sum(q_f * n_prev, axis=-1, keepdims=True)
        out_ref[0, :, h * dh:(h + 1) * dh] = num / jnp.maximum(jnp.abs(den), jnp.exp(-m_t))
        e_col = e_end[:, j:j + 1]
        c_loc = _dot_tn((v_f * e_col).astype(BF16), k_b)
        n_loc = jnp.sum(k_f * e_col, axis=0, keepdims=True)
        a_j = a_row[:, j:j + 1]
        c_j = c_row[:, j:j + 1]
        c_ref[j] = a_j * c_prev + c_j * c_loc
        n_ref[j] = a_j * n_prev + c_j * n_loc


def _mlstm_kernel(zf_ref, zfp_ref, zfn_ref, zb_ref, zbp_ref, zbn_ref, vf_ref, vb_ref, gf_ref, gb_ref,
                  cw_ref, hf_ref, hb_ref, c_ref, n_ref, m_ref, padf_ref, padb_ref):
    c = pl.program_id(1)
    nc = pl.num_programs(1)

    @pl.when(c == 0)
    def _():
        c_ref[...] = jnp.zeros_like(c_ref)
        n_ref[...] = jnp.zeros_like(n_ref)
        m_ref[...] = jnp.zeros_like(m_ref)

    cw = cw_ref[...]
    qk_f = _conv_silu(zf_ref, zfp_ref, zfn_ref, c > 0, c < nc - 1, cw, padf_ref)
    qk_b = _conv_silu(zb_ref, zbp_ref, zbn_ref, c < nc - 1, c > 0, cw, padb_ref)
    g_f = gf_ref[0]
    g_b = gb_ref[0]
    _mlstm_direction(qk_f, vf_ref, g_f[:, :LANES], g_f[:, LANES:], 0, False, hf_ref, c_ref, n_ref, m_ref)
    _mlstm_direction(qk_b, vb_ref, g_b[:, :LANES], g_b[:, LANES:], MLSTM_HEADS, True, hb_ref,
                     c_ref, n_ref, m_ref)


def _mlstm(zqk, zv, zg, conv_w):
    B, S, _ = zqk.shape
    L = MLSTM_CHUNK
    nc = S // L
    w = MLSTM_WIDTH
    dh = MLSTM_HEAD_DIM
    hr = 16
    per = L // hr
    nhb = S // hr
    fwd = lambda b, c: (b, c, 0)
    bwd = lambda b, c: (b, nc - 1 - c, 0)
    f_prev = lambda b, c: (b, jnp.maximum(c * per - 1, 0), 0)
    f_next = lambda b, c: (b, jnp.minimum((c + 1) * per, nhb - 1), 0)
    b_prev = lambda b, c: (b, jnp.maximum((nc - 1 - c) * per - 1, 0), 0)
    b_next = lambda b, c: (b, jnp.minimum((nc - c) * per, nhb - 1), 0)
    return pl.pallas_call(
        _mlstm_kernel,
        grid=(B, nc),
        in_specs=[pl.BlockSpec((1, L, 2 * w), fwd), pl.BlockSpec((1, hr, 2 * w), f_prev),
                  pl.BlockSpec((1, hr, 2 * w), f_next),
                  pl.BlockSpec((1, L, 2 * w), bwd), pl.BlockSpec((1, hr, 2 * w), b_prev),
                  pl.BlockSpec((1, hr, 2 * w), b_next),
                  pl.BlockSpec((1, L, w), fwd), pl.BlockSpec((1, L, w), bwd),
                  pl.BlockSpec((1, L, 2 * LANES), fwd), pl.BlockSpec((1, L, 2 * LANES), bwd),
                  _const_spec(conv_w.shape)],
        out_specs=[pl.BlockSpec((1, L, w), fwd), pl.BlockSpec((1, L, w), bwd)],
        out_shape=[jax.ShapeDtypeStruct((B, S, w), F32)] * 2,
        scratch_shapes=[pltpu.VMEM((2 * MLSTM_HEADS, dh, dh), F32),
                        pltpu.VMEM((2 * MLSTM_HEADS, 1, dh), F32),
                        pltpu.VMEM((1, LANES), F32),
                        pltpu.VMEM((L + 16, 2 * w), F32),
                        pltpu.VMEM((L + 16, 2 * w), F32)],
        compiler_params=_cparams(("arbitrary", "arbitrary")),
        name="mlstm",
    )(zqk, zqk, zqk, zqk, zqk, zqk, zv, zv, zg, zg, conv_w)


def _mix_xattn_kernel(x_ref, ya_ref, hf_ref, hb_ref, zo_ref, mn_ref, wom_ref, wol_ref, xn_ref, wxq_ref,
                      kx_ref, vx_ref, wxo_ref, o_ref, *, x_scale):
    dh = MLSTM_HEAD_DIM
    hs = hf_ref[...] + hb_ref[...]
    parts = []
    for h in range(MLSTM_HEADS):
        sl = slice(h * dh, (h + 1) * dh)
        parts.append(_rms(hs[:, sl], mn_ref[:, sl]))
    hn = jnp.concatenate(parts, axis=-1)
    yl = (jax.nn.sigmoid(zo_ref[...].astype(F32)) * hn).astype(BF16)
    x1 = x_ref[...] + _dot(ya_ref[...], wom_ref[...]) + _dot(yl, wol_ref[...])
    hq = _rms(x1, xn_ref[...]).astype(BF16)
    q = (_dot(hq, wxq_ref[...]) * x_scale).astype(BF16)
    xd = q.shape[-1] // X_HEADS
    outs = []
    for h in range(X_HEADS):
        sl = slice(h * xd, (h + 1) * xd)
        s = _dot_nt(q[:, sl], kx_ref[0, :, sl])
        p = jnp.exp(s - jnp.max(s, axis=-1, keepdims=True))
        l = jnp.sum(p, axis=-1, keepdims=True)
        outs.append((_dot(p.astype(BF16), vx_ref[0, :, sl]) * (1.0 / l)).astype(BF16))
    o = jnp.concatenate(outs, axis=-1)
    o_ref[...] = x1 + _dot(o, wxo_ref[...])


def _mix_xattn(x2, ya, hf, hb, zo, mlstm_norm, wom, wol, xattn_norm, wxq, kx, vx, wxo, tm, S):
    T, D = x2.shape
    M = kx.shape[1]
    per_b = S // tm
    row = lambda n: pl.BlockSpec((tm, n), lambda i: (i, 0))
    mem_spec = pl.BlockSpec((1, M, D), lambda i: (i // per_b, 0, 0))
    x_scale = float((D // X_HEADS) ** -0.5)
    return pl.pallas_call(
        functools.partial(_mix_xattn_kernel, x_scale=x_scale),
        grid=(T // tm,),
        in_specs=[row(D), row(ya.shape[1]), row(hf.shape[1]), row(hb.shape[1]), row(zo.shape[1]),
                  _const_spec(mlstm_norm.shape), _const_spec(wom.shape), _const_spec(wol.shape),
                  _const_spec(xattn_norm.shape), _const_spec(wxq.shape), mem_spec, mem_spec,
                  _const_spec(wxo.shape)],
        out_specs=row(D),
        out_shape=jax.ShapeDtypeStruct((T, D), F32),
        compiler_params=_cparams(("arbitrary",)),
        name="mix_xattn",
    )(x2, ya, hf, hb, zo, mlstm_norm, wom, wol, xattn_norm, wxq, kx, vx, wxo)


def _ffn_kernel(x_ref, fn_ref, wg_ref, wu_ref, wd_ref, on_ref, o_ref, *, fc):
    x = x_ref[...]
    hn = _rms(x, fn_ref[...]).astype(BF16)
    dff = wg_ref.shape[1]
    acc = x
    for j in range(dff // fc):
        sl = slice(j * fc, (j + 1) * fc)
        g = _dot(hn, wg_ref[:, sl])
        u = _dot(hn, wu_ref[:, sl])
        acc = acc + _dot((g * jax.nn.sigmoid(g) * u).astype(BF16), wd_ref[sl, :])
    o_ref[...] = _rms(acc, on_ref[...])


def _ffn(x2, ffn_norm, wg, wu, wd, final_norm, tm, fc):
    T, D = x2.shape
    row = pl.BlockSpec((tm, D), lambda i: (i, 0))
    return pl.pallas_call(
        functools.partial(_ffn_kernel, fc=fc),
        grid=(T // tm,),
        in_specs=[row, _const_spec(ffn_norm.shape), _const_spec(wg.shape), _const_spec(wu.shape),
                  _const_spec(wd.shape), _const_spec(final_norm.shape)],
        out_specs=row,
        out_shape=jax.ShapeDtypeStruct((T, D), F32),
        compiler_params=_cparams(("arbitrary",)),
        name="ffn",
    )(x2, ffn_norm, wg, wu, wd, final_norm)


def _head_blocks(wmat, per_head, take, put):
    rows = wmat.shape[0]
    w3 = wmat.reshape(rows, MLA_HEADS, per_head)[:, :, take[0]:take[1]]
    out = jnp.zeros((rows, MLA_HEADS, LANES), wmat.dtype)
    out = out.at[:, :, put:put + (take[1] - take[0])].set(w3)
    return out.reshape(rows, MLA_HEADS * LANES)


def _lane_block(wmat, put):
    rows, n = wmat.shape
    return jnp.zeros((rows, LANES), wmat.dtype).at[:, put:put + n].set(wmat)


def kernel(x, mem, positions, attn_norm, w_in, q_norm, w_uq, kv_norm, w_ukv, mlstm_conv, mlstm_gate_bias,
           mlstm_norm, w_out, xattn_norm, mem_norm, w_xq, w_xkv, w_xo, ffn_norm, w_gate_up, w_down,
           final_norm):
    B, S, D = x.shape
    T = B * S
    assert w_in.shape[0] == 1, "single-layer problem: the final norm is fused into the SwiGLU kernel"
    l = 0
    tm = 512
    x2 = x.reshape(T, D)

    inv = ROPE_THETA ** (-jnp.arange(0, ROPE_DIM, 2, dtype=F32) / ROPE_DIM)
    ang = positions.astype(F32).reshape(T, 1) * inv
    cos, sin = jnp.cos(ang), jnp.sin(ang)
    pad = jnp.zeros((T, LANES - NOPE_DIM - ROPE_DIM), F32)
    ctab = jnp.concatenate([jnp.ones((T, NOPE_DIM), F32), cos, cos, pad], axis=-1)
    stab = jnp.concatenate([jnp.zeros((T, NOPE_DIM), F32), -sin, sin, pad], axis=-1)
    vone = jnp.tile(jnp.zeros((1, LANES), F32).at[0, V_DIM].set(1.0), (1, MLA_HEADS))

    o_cq, o_ckv, o_kr = 0, Q_LORA, Q_LORA + KV_LORA
    o_qk = o_kr + ROPE_DIM
    o_v = o_qk + 2 * MLSTM_WIDTH
    o_o = o_v + MLSTM_WIDTH
    o_g = o_o + MLSTM_WIDTH
    nh = MLSTM_HEADS
    qd = NOPE_DIM + ROPE_DIM
    wi = w_in[l]
    kr1 = wi[:, o_kr:o_kr + HALF_ROPE]
    kr2 = wi[:, o_kr + HALF_ROPE:o_qk]
    kr_a = _lane_block(jnp.concatenate([kr1, kr2], axis=1), NOPE_DIM)
    kr_b = _lane_block(jnp.concatenate([kr2, kr1], axis=1), NOPE_DIM)
    wa = jnp.concatenate([wi[:, o_cq:o_kr], kr_a, kr_b], axis=1).astype(BF16)
    wg = wi[:, o_g:]
    wgi = _lane_block(jnp.concatenate([wg[:, 0:nh], wg[:, 2 * nh:3 * nh]], axis=1), 0)
    wgf = _lane_block(jnp.concatenate([wg[:, nh:2 * nh], wg[:, 3 * nh:4 * nh]], axis=1), 0)
    wb = jnp.concatenate([wi[:, o_qk:o_g], wgi, wgf], axis=1).astype(BF16)
    gb = mlstm_gate_bias[l][None, :]
    gbias = jnp.concatenate(
        [_lane_block(jnp.concatenate([gb[:, 0:nh], gb[:, 2 * nh:3 * nh]], axis=1), 0),
         _lane_block(jnp.concatenate([gb[:, nh:2 * nh], gb[:, 3 * nh:4 * nh]], axis=1), 0)], axis=1)
    wq = w_uq[l]
    wq_a = (_head_blocks(wq, qd, (0, NOPE_DIM), 0)
            + _head_blocks(wq, qd, (NOPE_DIM, qd), NOPE_DIM))
    wq_b = (_head_blocks(wq, qd, (NOPE_DIM + HALF_ROPE, qd), NOPE_DIM)
            + _head_blocks(wq, qd, (NOPE_DIM, NOPE_DIM + HALF_ROPE), NOPE_DIM + HALF_ROPE))
    wqq = jnp.concatenate([wq_a, wq_b], axis=1).astype(BF16)
    wkv = w_ukv[l]
    kvd = NOPE_DIM + V_DIM
    wkv2 = jnp.concatenate([_head_blocks(wkv, kvd, (0, NOPE_DIM), 0),
                            _head_blocks(wkv, kvd, (NOPE_DIM, kvd), 0)], axis=1).astype(BF16)
    wo = w_out[l]
    wom = wo[:MLA_HEADS * V_DIM].astype(BF16)
    wol = wo[MLA_HEADS * V_DIM:].astype(BF16)
    dff = w_down.shape[1]

    kx, vx = _mem_kv(mem, mem_norm[l][None, :], w_xkv[l].astype(BF16))
    q, k, v, zqk, zv, zo, zg = _in_proj(
        x2, ctab, stab, attn_norm[l][None, :], wa, wb, q_norm[l][None, :], wqq, kv_norm[l][None, :],
        wkv2, gbias, vone, tm)
    hw = MLA_HEADS * LANES
    ya = _mla_attention(q.reshape(B, S, hw), k.reshape(B, S, hw), v.reshape(B, S, hw), 512, 512)
    hf, hb = _mlstm(zqk.reshape(B, S, -1), zv.reshape(B, S, -1), zg.reshape(B, S, -1), mlstm_conv[l])
    x2 = _mix_xattn(x2, ya.reshape(T, -1), hf.reshape(T, -1), hb.reshape(T, -1), zo,
                    mlstm_norm[l][None, :], wom, wol, xattn_norm[l][None, :], w_xq[l].astype(BF16),
                    kx, vx, w_xo[l].astype(BF16), tm, S)
    wgu = w_gate_up[l].astype(BF16)
    y = _ffn(x2, ffn_norm[l][None, :], wgu[:, :dff], wgu[:, dff:], w_down[l].astype(BF16),
             final_norm[None, :], tm, 256)
    return y.reshape(B, S, D)
```

```python
import functools

import numpy as np
import jax
import jax.numpy as jnp
from jax import lax
from jax.experimental import pallas as pl
from jax.experimental.pallas import tpu as pltpu

F32 = jnp.float32
BF16 = jnp.bfloat16

EPS = 1e-6
MLA_HEADS = 8
Q_LORA = 256
KV_LORA = 128
NOPE_DIM = 64
ROPE_DIM = 32
V_DIM = 64
ROPE_THETA = 10000.0
MLSTM_HEADS = 4
MLSTM_HEAD_DIM = 128
MLSTM_WIDTH = MLSTM_HEADS * MLSTM_HEAD_DIM
MLSTM_CHUNK = 128
X_HEADS = 4
LANES = 128
HALF_ROPE = ROPE_DIM // 2

VMEM_LIMIT = 56 * 1024 * 1024


def _cparams(sem):
    return pltpu.CompilerParams(dimension_semantics=sem, vmem_limit_bytes=VMEM_LIMIT)


def _rms(x, g):
    ms = jnp.mean(x * x, axis=-1, keepdims=True)
    return x * lax.rsqrt(ms + EPS) * g


def _dot(a, b):
    return jnp.dot(a, b, preferred_element_type=F32)


def _dot_nt(a, b):
    return lax.dot_general(a, b, (((1,), (1,)), ((), ())), preferred_element_type=F32)


def _dot_tn(a, b):
    return lax.dot_general(a, b, (((0,), (0,)), ((), ())), preferred_element_type=F32)


def _const_spec(shape):
    return pl.BlockSpec(shape, lambda *_: (0,) * len(shape))


def _mem_kv_kernel(mem_ref, g_ref, w_ref, k_ref, v_ref):
    d = mem_ref.shape[-1]
    mn = _rms(mem_ref[0], g_ref[...]).astype(BF16)
    kv = _dot(mn, w_ref[...])
    k_ref[0] = kv[:, :d].astype(BF16)
    v_ref[0] = kv[:, d:].astype(BF16)


def _mem_kv(mem, mem_norm, w_xkv):
    B, M, D = mem.shape
    return pl.pallas_call(
        _mem_kv_kernel,
        grid=(B,),
        in_specs=[pl.BlockSpec((1, M, D), lambda b: (b, 0, 0)),
                  _const_spec((1, D)), _const_spec((D, 2 * D))],
        out_specs=[pl.BlockSpec((1, M, D), lambda b: (b, 0, 0))] * 2,
        out_shape=[jax.ShapeDtypeStruct((B, M, D), BF16)] * 2,
        compiler_params=_cparams(("arbitrary",)),
        name="mem_kv",
    )(mem, mem_norm, w_xkv)


HALO = 16
N_DIRS = 2
N_GATE_COLS = N_DIRS * MLSTM_HEADS


def _log_sigmoid(x):
    return jnp.minimum(x, 0.0) - jnp.log(1.0 + jnp.exp(-jnp.abs(x)))


def _chunk_scan(x, op, ident):
    n = x.shape[1]
    rows = lax.broadcasted_iota(jnp.int32, x.shape, 0)
    lanes = lax.broadcasted_iota(jnp.int32, x.shape, 1)
    fwd = rows < MLSTM_HEADS
    k = 1
    while k < n:
        down = pltpu.roll(x, k, axis=1)
        up = pltpu.roll(x, n - k, axis=1)
        shifted = jnp.where(fwd, jnp.where(lanes >= k, down, ident), jnp.where(lanes < n - k, up, ident))
        x = op(x, shifted)
        k *= 2
    return x


def _inproj_kernel(x_ref, xp_ref, xn_ref, ct_ref, st_ref, an_ref, wa_ref, wb_ref, qn_ref, wq_ref, kvn_ref,
                   wkv_ref, gb_ref, vone_ref, cw_ref,
                   q_ref, k_ref, v_ref, qm_ref, kt_ref, zv_ref, zo_ref, gc_ref, gr_ref,
                   *, q_scale, k_scale, tiles_per_seq):
    hw = MLA_HEADS * LANES
    tm = x_ref.shape[0]
    L = MLSTM_CHUNK
    w = MLSTM_WIDTH
    i = pl.program_id(0)
    first = (i % tiles_per_seq) == 0
    last = (i % tiles_per_seq) == tiles_per_seq - 1
    x_ext = jnp.concatenate([xp_ref[...], x_ref[...], xn_ref[...]], axis=0)
    h_ext = _rms(x_ext, an_ref[...]).astype(BF16)
    h = h_ext[HALO:HALO + tm]
    za = _dot(h, wa_ref[...])
    cq = za[:, :Q_LORA]
    ckv = za[:, Q_LORA:Q_LORA + KV_LORA]
    kra = za[:, Q_LORA + KV_LORA:Q_LORA + KV_LORA + LANES]
    krb = za[:, Q_LORA + KV_LORA + LANES:Q_LORA + KV_LORA + 2 * LANES]

    gt = (za[:, Q_LORA + KV_LORA + 2 * LANES:] + gb_ref[...]).T
    ng = N_GATE_COLS
    pad = jnp.zeros((L - ng, L), F32)
    for c in range(tm // L):
        cs = slice(c * L, (c + 1) * L)
        ic = gt[0:ng, cs]
        logf = _log_sigmoid(gt[ng:2 * ng, cs])
        bc = _chunk_scan(logf, jnp.add, 0.0)
        uc = ic - bc
        cmc = _chunk_scan(uc, jnp.maximum, -jnp.inf)
        ec = jnp.exp(uc - jnp.max(uc, axis=1, keepdims=True))
        gr_ref[0:ng, cs] = uc
        gr_ref[ng:2 * ng, cs] = ec
        gc_ref[cs, 0:LANES] = jnp.concatenate([cmc, pad], axis=0).T
        gc_ref[cs, LANES:2 * LANES] = jnp.concatenate([bc, pad], axis=0).T

    zqk = _dot(h_ext, wb_ref[:, :2 * w])
    zc = zqk[HALO:HALO + tm]
    z_all = jnp.concatenate([jnp.where(first, 0.0, zqk[:HALO]), zc, jnp.where(last, 0.0, zqk[HALO + tm:])],
                            axis=0)
    n_ext = tm + 2 * HALO
    zp = pltpu.roll(z_all, 1, axis=0)[HALO:HALO + tm]
    zn = pltpu.roll(z_all, n_ext - 1, axis=0)[HALO:HALO + tm]
    cw = cw_ref[...]
    y = zp * cw[0:1, :] + zc * cw[1:2, :]
    y = y + zn * cw[2:3, :]
    qk = y * jax.nn.sigmoid(y)
    qm_ref[...] = qk[:, :w].astype(BF16)
    kt_ref[...] = (qk[:, w:] * k_scale).T.astype(BF16)

    ct = ct_ref[...]
    st = st_ref[...]
    cqn = _rms(cq, qn_ref[...]).astype(BF16)
    ckvn = _rms(ckv, kvn_ref[...]).astype(BF16)
    qq = _dot(cqn, wq_ref[...])
    q = qq[:, :hw] * jnp.tile(ct, (1, MLA_HEADS)) + qq[:, hw:] * jnp.tile(st, (1, MLA_HEADS))
    q_ref[...] = (q * q_scale).astype(BF16)
    kv = _dot(ckvn, wkv_ref[...])
    krot = kra * ct + krb * st
    k_ref[...] = (kv[:, :hw] + jnp.tile(krot, (1, MLA_HEADS))).astype(BF16)
    v_ref[...] = (kv[:, hw:] + vone_ref[...]).astype(BF16)

    zvo = _dot(h, wb_ref[:, 2 * w:])
    zv_ref[...] = zvo[:, :w].astype(BF16)
    zo_ref[...] = zvo[:, w:].astype(BF16)


def _in_proj(x2, ctab, stab, attn_norm, wa, wb, q_norm, wq, kv_norm, wkv, gbias, vone, conv_w, tm, S):
    T, D = x2.shape
    hw = MLA_HEADS * LANES
    w = MLSTM_WIDTH
    per = tm // HALO
    nhb = T // HALO
    row = lambda n: pl.BlockSpec((tm, n), lambda i: (i, 0))
    col = lambda n: pl.BlockSpec((n, tm), lambda i: (0, i))
    prev = pl.BlockSpec((HALO, D), lambda i: (jnp.maximum(i * per - 1, 0), 0))
    nxt = pl.BlockSpec((HALO, D), lambda i: (jnp.minimum((i + 1) * per, nhb - 1), 0))
    out_specs = [row(hw), row(hw), row(hw), row(w), col(w), row(w), row(w), row(2 * LANES),
                 col(2 * N_GATE_COLS)]
    out_shape = [jax.ShapeDtypeStruct((T, hw), BF16)] * 3 + [
        jax.ShapeDtypeStruct((T, w), BF16), jax.ShapeDtypeStruct((w, T), BF16),
        jax.ShapeDtypeStruct((T, w), BF16), jax.ShapeDtypeStruct((T, w), BF16),
        jax.ShapeDtypeStruct((T, 2 * LANES), F32), jax.ShapeDtypeStruct((2 * N_GATE_COLS, T), F32)]
    q_scale = float((NOPE_DIM + ROPE_DIM) ** -0.5 * np.log2(np.e))
    return pl.pallas_call(
        functools.partial(_inproj_kernel, q_scale=q_scale, k_scale=float(MLSTM_HEAD_DIM ** -0.5),
                          tiles_per_seq=S // tm),
        grid=(T // tm,),
        in_specs=[row(D), prev, nxt, row(LANES), row(LANES), _const_spec(attn_norm.shape),
                  _const_spec(wa.shape), _const_spec(wb.shape), _const_spec(q_norm.shape),
                  _const_spec(wq.shape), _const_spec(kv_norm.shape), _const_spec(wkv.shape),
                  _const_spec(gbias.shape), _const_spec(vone.shape), _const_spec(conv_w.shape)],
        out_specs=out_specs,
        out_shape=out_shape,
        compiler_params=_cparams(("arbitrary",)),
        name="in_proj",
    )(x2, x2, x2, ctab, stab, attn_norm, wa, wb, q_norm, wq, kv_norm, wkv, gbias, vone, conv_w)


HEADS_PER_STEP = 2
V_ROWS = 80


def _mla_attn_kernel(q_ref, k_ref, v_ref, o_ref, vt_ref, qt_ref, s_ref, mc_ref, acc_ref, m_ref, *, tk):
    S = k_ref.shape[1]
    tq = q_ref.shape[1]
    nk = S // tk
    assert nk >= 2 and nk % 2 == 0
    heads = range(HEADS_PER_STEP)

    @pl.when(pl.program_id(2) == 0)
    def _():
        for h in heads:
            for c in range(nk):
                blk = v_ref[0, c * tk:(c + 1) * tk, h * LANES:(h + 1) * LANES].astype(F32)
                vt_ref[h, :, c * tk:(c + 1) * tk] = blk.T.astype(BF16)

    for h in heads:
        qt_ref[h] = q_ref[0, :, h * LANES:(h + 1) * LANES].astype(F32).T.astype(BF16)
        m_ref[h] = jnp.full((1, tq), -jnp.inf, F32)
        acc_ref[h] = jnp.zeros((V_ROWS, tq), F32)

    def scores(off, slot):
        for h in heads:
            st = _dot(k_ref[0, pl.ds(off, tk), h * LANES:(h + 1) * LANES], qt_ref[h])
            s_ref[slot, h] = st
            mc_ref[slot, h] = jnp.max(st, axis=0, keepdims=True)

    def consume(off, slot):
        for h in heads:
            m_old = m_ref[h]
            m_new = jnp.maximum(m_old, mc_ref[slot, h])
            alpha = jnp.exp2(m_old - m_new)
            p = jnp.exp2(s_ref[slot, h] - m_new).astype(BF16)
            acc_ref[h] = alpha * acc_ref[h] + _dot(vt_ref[h, 0:V_ROWS, pl.ds(off, tk)], p)
            m_ref[h] = m_new

    def step(c, slot):
        scores(pl.multiple_of((c + 1) * tk, tk), 1 - slot)
        consume(pl.multiple_of(c * tk, tk), slot)

    def body(jj, carry):
        step(2 * jj, 0)
        step(2 * jj + 1, 1)
        return carry

    scores(0, 0)
    lax.fori_loop(0, nk // 2 - 1, body, 0)
    step(nk - 2, 0)
    consume((nk - 1) * tk, 1)
    outs = []
    for h in heads:
        acc = acc_ref[h]
        o_t = acc[0:V_DIM, :] * (1.0 / acc[V_DIM:V_DIM + 1, :])
        outs.append(o_t.T)
    o_ref[0] = jnp.concatenate(outs, axis=-1).astype(BF16)


def _mla_attention(q, k, v, tq, tk):
    B, S, _ = q.shape
    hp = HEADS_PER_STEP
    return pl.pallas_call(
        functools.partial(_mla_attn_kernel, tk=tk),
        grid=(B, MLA_HEADS // hp, S // tq),
        in_specs=[pl.BlockSpec((1, tq, hp * LANES), lambda b, h, i: (b, i, h)),
                  pl.BlockSpec((1, S, hp * LANES), lambda b, h, i: (b, 0, h)),
                  pl.BlockSpec((1, S, hp * LANES), lambda b, h, i: (b, 0, h))],
        out_specs=pl.BlockSpec((1, tq, hp * V_DIM), lambda b, h, i: (b, i, h)),
        out_shape=jax.ShapeDtypeStruct((B, S, MLA_HEADS * V_DIM), BF16),
        scratch_shapes=[pltpu.VMEM((hp, LANES, S), BF16),
                        pltpu.VMEM((hp, LANES, tq), BF16),
                        pltpu.VMEM((2, hp, tk, tq), F32),
                        pltpu.VMEM((2, hp, 1, tq), F32),
                        pltpu.VMEM((hp, V_ROWS, tq), F32),
                        pltpu.VMEM((hp, 1, tq), F32)],
        compiler_params=_cparams(("arbitrary", "arbitrary", "arbitrary")),
        name="mla_attn",
    )(q, k, v)


def _lane_bcast(x, j):
    return jnp.broadcast_to(x[:, j:j + 1], x.shape)


def _mlstm_direction(q_ref, kt_ref, v_ref, gc_ref, gr_ref, col0, rev, out_ref, ct_ref, m_ref):
    L = MLSTM_CHUNK
    dh = MLSTM_HEAD_DIM
    gc = gc_ref[0]
    cm = gc[:, :LANES]
    b = gc[:, LANES:]
    gr = gr_ref[...]
    m_row = m_ref[...]
    g = jnp.maximum(m_row, cm)
    floor = jnp.exp(-(b + g))
    end = 0 if rev else L - 1
    blast = b[end:end + 1, :]
    m_loc = blast + cm[end:end + 1, :]
    m_new = jnp.maximum(blast + m_row, m_loc)
    a_row = jnp.exp(blast + m_row - m_new)
    c_row = jnp.exp(m_loc - m_new)
    lane1 = lax.broadcasted_iota(jnp.int32, m_row.shape, 1)
    m_ref[...] = jnp.where((lane1 >= col0) & (lane1 < col0 + MLSTM_HEADS), m_new, m_row)
    rows = lax.broadcasted_iota(jnp.int32, (L, L), 0)
    cols = lax.broadcasted_iota(jnp.int32, (L, L), 1)
    mask = (cols >= rows) if rev else (cols <= rows)
    for h in range(MLSTM_HEADS):
        j = col0 + h
        hs = slice(h * dh, (h + 1) * dh)
        q = q_ref[0, :, hs]
        kt = kt_ref[hs, :]
        v_ext = jnp.concatenate([v_ref[0, :, hs], jnp.where(cols == j, 1.0, 0.0).astype(BF16)], axis=1)
        g_rep = _lane_bcast(g, j)
        a_rep = jnp.exp(m_row[:, j:j + 1] - g_rep)
        p = jnp.exp(jnp.where(mask, gr[j:j + 1, :] - g_rep, -jnp.inf))
        s = (_dot(q, kt) * p).astype(BF16)
        tot = _dot(s, v_ext) + jnp.concatenate([a_rep, a_rep], axis=1) * _dot(q, ct_ref[j].astype(BF16))
        r = 1.0 / jnp.maximum(jnp.abs(tot[:, LANES:]), floor)
        out_ref[0, :, hs] = tot[:, :LANES] * _lane_bcast(r, j)
        kte = (kt.astype(F32) * gr[N_GATE_COLS + j:N_GATE_COLS + j + 1, :]).astype(BF16)
        ct_ref[j] = a_row[:, j:j + 1] * ct_ref[j] + c_row[:, j:j + 1] * _dot(kte, v_ext)


def _mlstm_kernel(qf_ref, ktf_ref, vf_ref, gcf_ref, grf_ref, qb_ref, ktb_ref, vb_ref, gcb_ref, grb_ref,
                  hf_ref, hb_ref, ct_ref, m_ref):
    @pl.when(pl.program_id(1) == 0)
    def _():
        ct_ref[...] = jnp.zeros_like(ct_ref)
        m_ref[...] = jnp.zeros_like(m_ref)

    _mlstm_direction(qf_ref, ktf_ref, vf_ref, gcf_ref, grf_ref, 0, False, hf_ref, ct_ref, m_ref)
    _mlstm_direction(qb_ref, ktb_ref, vb_ref, gcb_ref, grb_ref, MLSTM_HEADS, True, hb_ref, ct_ref, m_ref)


def _mlstm(qm, kt, zv, gc, gr):
    B, S, w = qm.shape
    L = MLSTM_CHUNK
    nc = S // L
    dh = MLSTM_HEAD_DIM
    fwd = lambda b, c: (b, c, 0)
    bwd = lambda b, c: (b, nc - 1 - c, 0)
    fwd_t = lambda b, c: (0, b * nc + c)
    bwd_t = lambda b, c: (0, b * nc + nc - 1 - c)
    ng2 = 2 * N_GATE_COLS

    def specs(im, im_t):
        return [pl.BlockSpec((1, L, w), im), pl.BlockSpec((w, L), im_t), pl.BlockSpec((1, L, w), im),
                pl.BlockSpec((1, L, 2 * LANES), im), pl.BlockSpec((ng2, L), im_t)]

    return pl.pallas_call(
        _mlstm_kernel,
        grid=(B, nc),
        in_specs=specs(fwd, fwd_t) + specs(bwd, bwd_t),
        out_specs=[pl.BlockSpec((1, L, w), fwd), pl.BlockSpec((1, L, w), bwd)],
        out_shape=[jax.ShapeDtypeStruct((B, S, w), F32)] * 2,
        scratch_shapes=[pltpu.VMEM((N_GATE_COLS, dh, 2 * LANES), F32),
                        pltpu.VMEM((1, LANES), F32)],
        compiler_params=_cparams(("arbitrary", "arbitrary")),
        name="mlstm",
    )(qm, kt, zv, gc, gr, qm, kt, zv, gc, gr)


def _mix_xattn_kernel(x_ref, ya_ref, hf_ref, hb_ref, zo_ref, mn_ref, wom_ref, wol_ref, xn_ref, wxq_ref,
                      kx_ref, vx_ref, wxo_ref, o_ref, *, x_scale):
    dh = MLSTM_HEAD_DIM
    hs = hf_ref[...] + hb_ref[...]
    parts = []
    for h in range(MLSTM_HEADS):
        sl = slice(h * dh, (h + 1) * dh)
        parts.append(_rms(hs[:, sl], mn_ref[:, sl]))
    hn = jnp.concatenate(parts, axis=-1)
    yl = (jax.nn.sigmoid(zo_ref[...].astype(F32)) * hn).astype(BF16)
    x1 = x_ref[...] + _dot(ya_ref[...], wom_ref[...]) + _dot(yl, wol_ref[...])
    hq = _rms(x1, xn_ref[...]).astype(BF16)
    q = (_dot(hq, wxq_ref[...]) * x_scale).astype(BF16)
    xd = q.shape[-1] // X_HEADS
    outs = []
    for h in range(X_HEADS):
        sl = slice(h * xd, (h + 1) * xd)
        s = _dot_nt(q[:, sl], kx_ref[0, :, sl])
        p = jnp.exp(s - jnp.max(s, axis=-1, keepdims=True))
        l = jnp.sum(p, axis=-1, keepdims=True)
        outs.append((_dot(p.astype(BF16), vx_ref[0, :, sl]) * (1.0 / l)).astype(BF16))
    o = jnp.concatenate(outs, axis=-1)
    o_ref[...] = x1 + _dot(o, wxo_ref[...])


def _mix_xattn(x2, ya, hf, hb, zo, mlstm_norm, wom, wol, xattn_norm, wxq, kx, vx, wxo, tm, S):
    T, D = x2.shape
    M = kx.shape[1]
    per_b = S // tm
    row = lambda n: pl.BlockSpec((tm, n), lambda i: (i, 0))
    mem_spec = pl.BlockSpec((1, M, D), lambda i: (i // per_b, 0, 0))
    x_scale = float((D // X_HEADS) ** -0.5)
    return pl.pallas_call(
        functools.partial(_mix_xattn_kernel, x_scale=x_scale),
        grid=(T // tm,),
        in_specs=[row(D), row(ya.shape[1]), row(hf.shape[1]), row(hb.shape[1]), row(zo.shape[1]),
                  _const_spec(mlstm_norm.shape), _const_spec(wom.shape), _const_spec(wol.shape),
                  _const_spec(xattn_norm.shape), _const_spec(wxq.shape), mem_spec, mem_spec,
                  _const_spec(wxo.shape)],
        out_specs=row(D),
        out_shape=jax.ShapeDtypeStruct((T, D), F32),
        compiler_params=_cparams(("arbitrary",)),
        name="mix_xattn",
    )(x2, ya, hf, hb, zo, mlstm_norm, wom, wol, xattn_norm, wxq, kx, vx, wxo)


def _ffn_kernel(x_ref, fn_ref, wg_ref, wu_ref, wd_ref, on_ref, o_ref, *, fc):
    x = x_ref[...]
    hn = _rms(x, fn_ref[...]).astype(BF16)
    dff = wg_ref.shape[1]
    acc = x
    for j in range(dff // fc):
        sl = slice(j * fc, (j + 1) * fc)
        g = _dot(hn, wg_ref[:, sl])
        u = _dot(hn, wu_ref[:, sl])
        acc = acc + _dot((g * jax.nn.sigmoid(g) * u).astype(BF16), wd_ref[sl, :])
    o_ref[...] = _rms(acc, on_ref[...])


def _ffn(x2, ffn_norm, wg, wu, wd, final_norm, tm, fc):
    T, D = x2.shape
    row = pl.BlockSpec((tm, D), lambda i: (i, 0))
    return pl.pallas_call(
        functools.partial(_ffn_kernel, fc=fc),
        grid=(T // tm,),
        in_specs=[row, _const_spec(ffn_norm.shape), _const_spec(wg.shape), _const_spec(wu.shape),
                  _const_spec(wd.shape), _const_spec(final_norm.shape)],
        out_specs=row,
        out_shape=jax.ShapeDtypeStruct((T, D), F32),
        compiler_params=_cparams(("arbitrary",)),
        name="ffn",
    )(x2, ffn_norm, wg, wu, wd, final_norm)


def _head_blocks(wmat, per_head, take, put):
    rows = wmat.shape[0]
    w3 = wmat.reshape(rows, MLA_HEADS, per_head)[:, :, take[0]:take[1]]
    out = jnp.zeros((rows, MLA_HEADS, LANES), wmat.dtype)
    out = out.at[:, :, put:put + (take[1] - take[0])].set(w3)
    return out.reshape(rows, MLA_HEADS * LANES)


def _lane_block(wmat, put):
    rows, n = wmat.shape
    return jnp.zeros((rows, LANES), wmat.dtype).at[:, put:put + n].set(wmat)


def kernel(x, mem, positions, attn_norm, w_in, q_norm, w_uq, kv_norm, w_ukv, mlstm_conv, mlstm_gate_bias,
           mlstm_norm, w_out, xattn_norm, mem_norm, w_xq, w_xkv, w_xo, ffn_norm, w_gate_up, w_down,
           final_norm):
    B, S, D = x.shape
    T = B * S
    assert w_in.shape[0] == 1, "single-layer problem: the final norm is fused into the SwiGLU kernel"
    l = 0
    tm = 512
    x2 = x.reshape(T, D)

    inv = ROPE_THETA ** (-jnp.arange(0, ROPE_DIM, 2, dtype=F32) / ROPE_DIM)
    ang = positions.astype(F32).reshape(T, 1) * inv
    cos, sin = jnp.cos(ang), jnp.sin(ang)
    pad = jnp.zeros((T, LANES - NOPE_DIM - ROPE_DIM), F32)
    ctab = jnp.concatenate([jnp.ones((T, NOPE_DIM), F32), cos, cos, pad], axis=-1)
    stab = jnp.concatenate([jnp.zeros((T, NOPE_DIM), F32), -sin, sin, pad], axis=-1)
    vone = jnp.tile(jnp.zeros((1, LANES), F32).at[0, V_DIM].set(1.0), (1, MLA_HEADS))

    o_cq, o_ckv, o_kr = 0, Q_LORA, Q_LORA + KV_LORA
    o_qk = o_kr + ROPE_DIM
    o_v = o_qk + 2 * MLSTM_WIDTH
    o_o = o_v + MLSTM_WIDTH
    o_g = o_o + MLSTM_WIDTH
    nh = MLSTM_HEADS
    qd = NOPE_DIM + ROPE_DIM
    wi = w_in[l]
    kr1 = wi[:, o_kr:o_kr + HALF_ROPE]
    kr2 = wi[:, o_kr + HALF_ROPE:o_qk]
    kr_a = _lane_block(jnp.concatenate([kr1, kr2], axis=1), NOPE_DIM)
    kr_b = _lane_block(jnp.concatenate([kr2, kr1], axis=1), NOPE_DIM)
    wg = wi[:, o_g:]
    gate_perm = lambda m: jnp.concatenate(
        [m[:, 0:nh], m[:, 2 * nh:3 * nh], m[:, nh:2 * nh], m[:, 3 * nh:4 * nh]], axis=1)
    wa = jnp.concatenate([wi[:, o_cq:o_kr], kr_a, kr_b, _lane_block(gate_perm(wg), 0)], axis=1).astype(BF16)
    wb = wi[:, o_qk:o_g].astype(BF16)
    gbias = _lane_block(gate_perm(mlstm_gate_bias[l][None, :]), 0)
    wq = w_uq[l]
    wq_a = (_head_blocks(wq, qd, (0, NOPE_DIM), 0)
            + _head_blocks(wq, qd, (NOPE_DIM, qd), NOPE_DIM))
    wq_b = (_head_blocks(wq, qd, (NOPE_DIM + HALF_ROPE, qd), NOPE_DIM)
            + _head_blocks(wq, qd, (NOPE_DIM, NOPE_DIM + HALF_ROPE), NOPE_DIM + HALF_ROPE))
    wqq = jnp.concatenate([wq_a, wq_b], axis=1).astype(BF16)
    wkv = w_ukv[l]
    kvd = NOPE_DIM + V_DIM
    wkv2 = jnp.concatenate([_head_blocks(wkv, kvd, (0, NOPE_DIM), 0),
                            _head_blocks(wkv, kvd, (NOPE_DIM, kvd), 0)], axis=1).astype(BF16)
    wo = w_out[l]
    wom = wo[:MLA_HEADS * V_DIM].astype(BF16)
    wol = wo[MLA_HEADS * V_DIM:].astype(BF16)
    dff = w_down.shape[1]

    kx, vx = _mem_kv(mem, mem_norm[l][None, :], w_xkv[l].astype(BF16))
    q, k, v, qm, kt, zv, zo, gc, gr = _in_proj(
        x2, ctab, stab, attn_norm[l][None, :], wa, wb, q_norm[l][None, :], wqq, kv_norm[l][None, :],
        wkv2, gbias, vone, mlstm_conv[l], tm, S)
    hw = MLA_HEADS * LANES
    ya = _mla_attention(q.reshape(B, S, hw), k.reshape(B, S, hw), v.reshape(B, S, hw), 512, 512)
    hf, hb = _mlstm(qm.reshape(B, S, -1), kt, zv.reshape(B, S, -1), gc.reshape(B, S, -1), gr)
    x2 = _mix_xattn(x2, ya.reshape(T, -1), hf.reshape(T, -1), hb.reshape(T, -1), zo,
                    mlstm_norm[l][None, :], wom, wol, xattn_norm[l][None, :], w_xq[l].astype(BF16),
                    kx, vx, w_xo[l].astype(BF16), tm, S)
    wgu = w_gate_up[l].astype(BF16)
    y = _ffn(x2, ffn_norm[l][None, :], wgu[:, :dff], wgu[:, dff:], w_down[l].astype(BF16),
             final_norm[None, :], tm, 256)
    return y.reshape(B, S, D)
```

```python
import functools

import numpy as np
import jax
import jax.numpy as jnp
from jax import lax
from jax.experimental import pallas as pl
from jax.experimental.pallas import tpu as pltpu

F32 = jnp.float32
BF16 = jnp.bfloat16

EPS = 1e-6
MLA_HEADS = 8
Q_LORA = 256
KV_LORA = 128
NOPE_DIM = 64
ROPE_DIM = 32
V_DIM = 64
ROPE_THETA = 10000.0
MLSTM_HEADS = 4
MLSTM_HEAD_DIM = 128
MLSTM_WIDTH = MLSTM_HEADS * MLSTM_HEAD_DIM
MLSTM_CHUNK = 128
X_HEADS = 4
LANES = 128
HALF_ROPE = ROPE_DIM // 2

VMEM_LIMIT = 56 * 1024 * 1024


def _cparams(sem):
    return pltpu.CompilerParams(dimension_semantics=sem, vmem_limit_bytes=VMEM_LIMIT)


def _rms(x, g):
    ms = jnp.mean(x * x, axis=-1, keepdims=True)
    return x * lax.rsqrt(ms + EPS) * g


def _dot(a, b):
    return jnp.dot(a, b, preferred_element_type=F32)


def _dot_nt(a, b):
    return lax.dot_general(a, b, (((1,), (1,)), ((), ())), preferred_element_type=F32)


def _dot_tn(a, b):
    return lax.dot_general(a, b, (((0,), (0,)), ((), ())), preferred_element_type=F32)


def _const_spec(shape):
    return pl.BlockSpec(shape, lambda *_: (0,) * len(shape))


def _mem_kv_kernel(mem_ref, g_ref, w_ref, k_ref, v_ref):
    d = mem_ref.shape[-1]
    mn = _rms(mem_ref[0], g_ref[...]).astype(BF16)
    kv = _dot(mn, w_ref[...])
    k_ref[0] = kv[:, :d].astype(BF16)
    v_ref[0] = kv[:, d:].astype(BF16)


def _mem_kv(mem, mem_norm, w_xkv):
    B, M, D = mem.shape
    return pl.pallas_call(
        _mem_kv_kernel,
        grid=(B,),
        in_specs=[pl.BlockSpec((1, M, D), lambda b: (b, 0, 0)),
                  _const_spec((1, D)), _const_spec((D, 2 * D))],
        out_specs=[pl.BlockSpec((1, M, D), lambda b: (b, 0, 0))] * 2,
        out_shape=[jax.ShapeDtypeStruct((B, M, D), BF16)] * 2,
        compiler_params=_cparams(("arbitrary",)),
        name="mem_kv",
    )(mem, mem_norm, w_xkv)


HALO = 16
N_DIRS = 2
N_GATE_COLS = N_DIRS * MLSTM_HEADS


def _log_sigmoid(x):
    return jnp.minimum(x, 0.0) - jnp.log(1.0 + jnp.exp(-jnp.abs(x)))


def _chunk_scan(x, op, ident):
    n = x.shape[1]
    rows = lax.broadcasted_iota(jnp.int32, x.shape, 0)
    lanes = lax.broadcasted_iota(jnp.int32, x.shape, 1)
    fwd = rows < MLSTM_HEADS
    k = 1
    while k < n:
        down = pltpu.roll(x, k, axis=1)
        up = pltpu.roll(x, n - k, axis=1)
        shifted = jnp.where(fwd, jnp.where(lanes >= k, down, ident), jnp.where(lanes < n - k, up, ident))
        x = op(x, shifted)
        k *= 2
    return x


def _inproj_kernel(x_ref, xp_ref, xn_ref, ct_ref, st_ref, an_ref, wa_ref, wb_ref, qn_ref, wq_ref, kvn_ref,
                   wkv_ref, gb_ref, vone_ref, cw_ref,
                   q_ref, k_ref, v_ref, qm_ref, kt_ref, zv_ref, zo_ref, gc_ref, gr_ref,
                   *, q_scale, k_scale, tiles_per_seq):
    hw = MLA_HEADS * LANES
    tm = x_ref.shape[0]
    L = MLSTM_CHUNK
    w = MLSTM_WIDTH
    i = pl.program_id(0)
    first = (i % tiles_per_seq) == 0
    last = (i % tiles_per_seq) == tiles_per_seq - 1
    x_ext = jnp.concatenate([xp_ref[...], x_ref[...], xn_ref[...]], axis=0)
    h_ext = _rms(x_ext, an_ref[...]).astype(BF16)
    h = h_ext[HALO:HALO + tm]
    za = _dot(h, wa_ref[...])
    cq = za[:, :Q_LORA]
    ckv = za[:, Q_LORA:Q_LORA + KV_LORA]
    kra = za[:, Q_LORA + KV_LORA:Q_LORA + KV_LORA + LANES]
    krb = za[:, Q_LORA + KV_LORA + LANES:Q_LORA + KV_LORA + 2 * LANES]

    gt = (za[:, Q_LORA + KV_LORA + 2 * LANES:] + gb_ref[...]).T
    ng = N_GATE_COLS
    pad = jnp.zeros((L - ng, L), F32)
    for c in range(tm // L):
        cs = slice(c * L, (c + 1) * L)
        ic = gt[0:ng, cs]
        logf = _log_sigmoid(gt[ng:2 * ng, cs])
        bc = _chunk_scan(logf, jnp.add, 0.0)
        uc = ic - bc
        cmc = _chunk_scan(uc, jnp.maximum, -jnp.inf)
        ec = jnp.exp(uc - jnp.max(uc, axis=1, keepdims=True))
        gr_ref[0:ng, cs] = uc
        gr_ref[ng:2 * ng, cs] = ec
        gc_ref[cs, 0:LANES] = jnp.concatenate([cmc, pad], axis=0).T
        gc_ref[cs, LANES:2 * LANES] = jnp.concatenate([bc, pad], axis=0).T

    zqk = _dot(h_ext, wb_ref[:, :2 * w])
    zc = zqk[HALO:HALO + tm]
    z_all = jnp.concatenate([jnp.where(first, 0.0, zqk[:HALO]), zc, jnp.where(last, 0.0, zqk[HALO + tm:])],
                            axis=0)
    n_ext = tm + 2 * HALO
    zp = pltpu.roll(z_all, 1, axis=0)[HALO:HALO + tm]
    zn = pltpu.roll(z_all, n_ext - 1, axis=0)[HALO:HALO + tm]
    cw = cw_ref[...]
    y = zp * cw[0:1, :] + zc * cw[1:2, :]
    y = y + zn * cw[2:3, :]
    qk = y * jax.nn.sigmoid(y)
    qm_ref[...] = qk[:, :w].astype(BF16)
    kt_ref[...] = (qk[:, w:] * k_scale).T.astype(BF16)

    ct = ct_ref[...]
    st = st_ref[...]
    cqn = _rms(cq, qn_ref[...]).astype(BF16)
    ckvn = _rms(ckv, kvn_ref[...]).astype(BF16)
    qq = _dot(cqn, wq_ref[...])
    q = qq[:, :hw] * jnp.tile(ct, (1, MLA_HEADS)) + qq[:, hw:] * jnp.tile(st, (1, MLA_HEADS))
    q_ref[...] = (q * q_scale).astype(BF16)
    kv = _dot(ckvn, wkv_ref[...])
    krot = kra * ct + krb * st
    k_ref[...] = (kv[:, :hw] + jnp.tile(krot, (1, MLA_HEADS))).astype(BF16)
    v_ref[...] = (kv[:, hw:] + vone_ref[...]).astype(BF16)

    zvo = _dot(h, wb_ref[:, 2 * w:])
    zv_ref[...] = zvo[:, :w].astype(BF16)
    zo_ref[...] = zvo[:, w:].astype(BF16)


def _in_proj(x2, ctab, stab, attn_norm, wa, wb, q_norm, wq, kv_norm, wkv, gbias, vone, conv_w, tm, S):
    T, D = x2.shape
    hw = MLA_HEADS * LANES
    w = MLSTM_WIDTH
    per = tm // HALO
    nhb = T // HALO
    row = lambda n: pl.BlockSpec((tm, n), lambda i: (i, 0))
    col = lambda n: pl.BlockSpec((n, tm), lambda i: (0, i))
    prev = pl.BlockSpec((HALO, D), lambda i: (jnp.maximum(i * per - 1, 0), 0))
    nxt = pl.BlockSpec((HALO, D), lambda i: (jnp.minimum((i + 1) * per, nhb - 1), 0))
    out_specs = [row(hw), row(hw), row(hw), row(w), col(w), row(w), row(w), row(2 * LANES),
                 col(2 * N_GATE_COLS)]
    out_shape = [jax.ShapeDtypeStruct((T, hw), BF16)] * 3 + [
        jax.ShapeDtypeStruct((T, w), BF16), jax.ShapeDtypeStruct((w, T), BF16),
        jax.ShapeDtypeStruct((T, w), BF16), jax.ShapeDtypeStruct((T, w), BF16),
        jax.ShapeDtypeStruct((T, 2 * LANES), F32), jax.ShapeDtypeStruct((2 * N_GATE_COLS, T), F32)]
    q_scale = float((NOPE_DIM + ROPE_DIM) ** -0.5 * np.log2(np.e))
    return pl.pallas_call(
        functools.partial(_inproj_kernel, q_scale=q_scale, k_scale=float(MLSTM_HEAD_DIM ** -0.5),
                          tiles_per_seq=S // tm),
        grid=(T // tm,),
        in_specs=[row(D), prev, nxt, row(LANES), row(LANES), _const_spec(attn_norm.shape),
                  _const_spec(wa.shape), _const_spec(wb.shape), _const_spec(q_norm.shape),
                  _const_spec(wq.shape), _const_spec(kv_norm.shape), _const_spec(wkv.shape),
                  _const_spec(gbias.shape), _const_spec(vone.shape), _const_spec(conv_w.shape)],
        out_specs=out_specs,
        out_shape=out_shape,
        compiler_params=_cparams(("arbitrary",)),
        name="in_proj",
    )(x2, x2, x2, ctab, stab, attn_norm, wa, wb, q_norm, wq, kv_norm, wkv, gbias, vone, conv_w)


HEADS_PER_STEP = 2
V_ROWS = 80


def _mla_attn_kernel(q_ref, qn_ref, k_ref, v_ref, o_ref, vt_ref, qt_ref, s_ref, mc_ref, acc_ref, m_ref,
                     *, tk, steps_per_iter):
    S = k_ref.shape[1]
    tq = q_ref.shape[1]
    nk = S // tk
    assert steps_per_iter % 2 == 0 and nk % steps_per_iter == 0
    heads = range(HEADS_PER_STEP)
    qi = pl.program_id(2)
    par = qi % 2

    def load_qt(src_ref, qslot):
        for h in heads:
            qt_ref[qslot, h] = src_ref[0, :, h * LANES:(h + 1) * LANES].astype(F32).T.astype(BF16)

    def scores(off, qslot, slot):
        for h in heads:
            st = _dot(k_ref[0, pl.ds(off, tk), h * LANES:(h + 1) * LANES], qt_ref[qslot, h])
            s_ref[slot, h] = st
            mc_ref[slot, h] = jnp.max(st, axis=0, keepdims=True)

    def consume(off, slot):
        for h in heads:
            m_old = m_ref[h]
            m_new = jnp.maximum(m_old, mc_ref[slot, h])
            alpha = jnp.exp2(m_old - m_new)
            p = jnp.exp2(s_ref[slot, h] - m_new).astype(BF16)
            acc_ref[h] = alpha * acc_ref[h] + _dot(vt_ref[h, 0:V_ROWS, pl.ds(off, tk)], p)
            m_ref[h] = m_new

    @pl.when(qi == 0)
    def _():
        for h in heads:
            for c in range(nk):
                blk = v_ref[0, c * tk:(c + 1) * tk, h * LANES:(h + 1) * LANES].astype(F32)
                vt_ref[h, :, c * tk:(c + 1) * tk] = blk.T.astype(BF16)
        load_qt(q_ref, 0)
        scores(0, 0, 0)

    load_qt(qn_ref, 1 - par)
    for h in heads:
        m_ref[h] = jnp.full((1, tq), -jnp.inf, F32)
        acc_ref[h] = jnp.zeros((V_ROWS, tq), F32)

    def step(c, slot):
        wrap = c + 1 == nk
        nxt = jnp.where(wrap, 0, c + 1)
        scores(pl.multiple_of(nxt * tk, tk), jnp.where(wrap, 1 - par, par), 1 - slot)
        consume(pl.multiple_of(c * tk, tk), slot)

    def body(jj, carry):
        for u in range(steps_per_iter):
            step(steps_per_iter * jj + u, u % 2)
        return carry

    lax.fori_loop(0, nk // steps_per_iter, body, 0)
    outs = []
    for h in heads:
        acc = acc_ref[h]
        o_t = acc[0:V_DIM, :] * (1.0 / acc[V_DIM:V_DIM + 1, :])
        outs.append(o_t.T)
    o_ref[0] = jnp.concatenate(outs, axis=-1).astype(BF16)


def _mla_attention(q, k, v, tq, tk, steps_per_iter):
    B, S, _ = q.shape
    hp = HEADS_PER_STEP
    nq = S // tq
    return pl.pallas_call(
        functools.partial(_mla_attn_kernel, tk=tk, steps_per_iter=steps_per_iter),
        grid=(B, MLA_HEADS // hp, nq),
        in_specs=[pl.BlockSpec((1, tq, hp * LANES), lambda b, h, i: (b, i, h)),
                  pl.BlockSpec((1, tq, hp * LANES), lambda b, h, i: (b, jnp.minimum(i + 1, nq - 1), h)),
                  pl.BlockSpec((1, S, hp * LANES), lambda b, h, i: (b, 0, h)),
                  pl.BlockSpec((1, S, hp * LANES), lambda b, h, i: (b, 0, h))],
        out_specs=pl.BlockSpec((1, tq, hp * V_DIM), lambda b, h, i: (b, i, h)),
        out_shape=jax.ShapeDtypeStruct((B, S, MLA_HEADS * V_DIM), BF16),
        scratch_shapes=[pltpu.VMEM((hp, LANES, S), BF16),
                        pltpu.VMEM((2, hp, LANES, tq), BF16),
                        pltpu.VMEM((2, hp, tk, tq), F32),
                        pltpu.VMEM((2, hp, 1, tq), F32),
                        pltpu.VMEM((hp, V_ROWS, tq), F32),
                        pltpu.VMEM((hp, 1, tq), F32)],
        compiler_params=_cparams(("arbitrary", "arbitrary", "arbitrary")),
        name="mla_attn",
    )(q, q, k, v)


def _lane_bcast(x, j):
    return jnp.broadcast_to(x[:, j:j + 1], x.shape)


def _mlstm_kernel(qf_ref, ktf_ref, vf_ref, gcf_ref, grf_ref, qb_ref, ktb_ref, vb_ref, gcb_ref, grb_ref,
                  hf_ref, hb_ref, ct_ref, m_ref):
    @pl.when(pl.program_id(1) == 0)
    def _():
        ct_ref[...] = jnp.zeros_like(ct_ref)
        m_ref[...] = jnp.zeros_like(m_ref)

    L = MLSTM_CHUNK
    dh = MLSTM_HEAD_DIM
    nh = MLSTM_HEADS
    rows = lax.broadcasted_iota(jnp.int32, (L, L), 0)
    cols = lax.broadcasted_iota(jnp.int32, (L, L), 1)
    m_row = m_ref[...]
    dirs = ((qf_ref, ktf_ref, vf_ref, gcf_ref, grf_ref, hf_ref, False),
            (qb_ref, ktb_ref, vb_ref, gcb_ref, grb_ref, hb_ref, True))

    gate = []
    for d, (_, _, _, gc_ref, gr_ref, _, rev) in enumerate(dirs):
        gc = gc_ref[0]
        cm = gc[:, :LANES]
        b = gc[:, LANES:]
        g = jnp.maximum(m_row, cm)
        floor = jnp.exp(-(b + g))
        end = 0 if rev else L - 1
        blast = b[end:end + 1, :]
        m_loc = blast + cm[end:end + 1, :]
        m_new = jnp.maximum(blast + m_row, m_loc)
        a_row = jnp.exp(blast + m_row - m_new)
        c_row = jnp.exp(m_loc - m_new)
        mask = (cols >= rows) if rev else (cols <= rows)
        gate.append((g, floor, a_row, c_row, m_new, mask, gr_ref[...]))
    lane1 = lax.broadcasted_iota(jnp.int32, m_row.shape, 1)
    m_ref[...] = jnp.where(lane1 < nh, gate[0][4], gate[1][4])

    units = [(d, h) for d in range(N_DIRS) for h in range(nh)]
    q_l, kt_l, vx_l, sc_l, qc_l = [], [], [], [], []
    for d, h in units:
        j = d * nh + h
        hs = slice(h * dh, (h + 1) * dh)
        q = dirs[d][0][0, :, hs]
        kt = dirs[d][1][hs, :]
        v_ext = jnp.concatenate([dirs[d][2][0, :, hs], jnp.where(cols == j, 1.0, 0.0).astype(BF16)], axis=1)
        q_l.append(q)
        kt_l.append(kt)
        vx_l.append(v_ext)
        sc_l.append(_dot(q, kt))
        qc_l.append(_dot(q, ct_ref[j].astype(BF16)))
    s_l, kte_l, a_l = [], [], []
    for u, (d, h) in enumerate(units):
        j = d * nh + h
        g, _, _, _, _, mask, gr = gate[d]
        g_rep = _lane_bcast(g, j)
        a_l.append(jnp.exp(m_row[:, j:j + 1] - g_rep))
        p = jnp.exp(jnp.where(mask, gr[j:j + 1, :] - g_rep, -jnp.inf))
        s_l.append((sc_l[u] * p).astype(BF16))
        kte_l.append((kt_l[u].astype(F32) * gr[N_GATE_COLS + j:N_GATE_COLS + j + 1, :]).astype(BF16))
    nd_l = [_dot(s_l[u], vx_l[u]) for u in range(len(units))]
    cl_l = [_dot(kte_l[u], vx_l[u]) for u in range(len(units))]
    for u, (d, h) in enumerate(units):
        j = d * nh + h
        _, floor, a_row, c_row, _, _, _ = gate[d]
        tot = nd_l[u] + jnp.concatenate([a_l[u], a_l[u]], axis=1) * qc_l[u]
        r = 1.0 / jnp.maximum(jnp.abs(tot[:, LANES:]), floor)
        dirs[d][5][0, :, h * dh:(h + 1) * dh] = tot[:, :LANES] * _lane_bcast(r, j)
        ct_ref[j] = a_row[:, j:j + 1] * ct_ref[j] + c_row[:, j:j + 1] * cl_l[u]


def _mlstm(qm, kt, zv, gc, gr):
    B, S, w = qm.shape
    L = MLSTM_CHUNK
    nc = S // L
    dh = MLSTM_HEAD_DIM
    fwd = lambda b, c: (b, c, 0)
    bwd = lambda b, c: (b, nc - 1 - c, 0)
    fwd_t = lambda b, c: (0, b * nc + c)
    bwd_t = lambda b, c: (0, b * nc + nc - 1 - c)
    ng2 = 2 * N_GATE_COLS

    def specs(im, im_t):
        return [pl.BlockSpec((1, L, w), im), pl.BlockSpec((w, L), im_t), pl.BlockSpec((1, L, w), im),
                pl.BlockSpec((1, L, 2 * LANES), im), pl.BlockSpec((ng2, L), im_t)]

    return pl.pallas_call(
        _mlstm_kernel,
        grid=(B, nc),
        in_specs=specs(fwd, fwd_t) + specs(bwd, bwd_t),
        out_specs=[pl.BlockSpec((1, L, w), fwd), pl.BlockSpec((1, L, w), bwd)],
        out_shape=[jax.ShapeDtypeStruct((B, S, w), F32)] * 2,
        scratch_shapes=[pltpu.VMEM((N_GATE_COLS, dh, 2 * LANES), F32),
                        pltpu.VMEM((1, LANES), F32)],
        compiler_params=_cparams(("arbitrary", "arbitrary")),
        name="mlstm",
    )(qm, kt, zv, gc, gr, qm, kt, zv, gc, gr)


def _mix_xattn_kernel(x_ref, ya_ref, hf_ref, hb_ref, zo_ref, mn_ref, wom_ref, wol_ref, xn_ref, wxq_ref,
                      kx_ref, vx_ref, wxo_ref, o_ref, *, x_scale):
    dh = MLSTM_HEAD_DIM
    hs = hf_ref[...] + hb_ref[...]
    parts = []
    for h in range(MLSTM_HEADS):
        sl = slice(h * dh, (h + 1) * dh)
        parts.append(_rms(hs[:, sl], mn_ref[:, sl]))
    hn = jnp.concatenate(parts, axis=-1)
    yl = (jax.nn.sigmoid(zo_ref[...].astype(F32)) * hn).astype(BF16)
    x1 = x_ref[...] + _dot(ya_ref[...], wom_ref[...]) + _dot(yl, wol_ref[...])
    hq = _rms(x1, xn_ref[...]).astype(BF16)
    q = (_dot(hq, wxq_ref[...]) * x_scale).astype(BF16)
    xd = q.shape[-1] // X_HEADS
    outs = []
    for h in range(X_HEADS):
        sl = slice(h * xd, (h + 1) * xd)
        s = _dot_nt(q[:, sl], kx_ref[0, :, sl])
        p = jnp.exp(s - jnp.max(s, axis=-1, keepdims=True))
        l = jnp.sum(p, axis=-1, keepdims=True)
        outs.append((_dot(p.astype(BF16), vx_ref[0, :, sl]) * (1.0 / l)).astype(BF16))
    o = jnp.concatenate(outs, axis=-1)
    o_ref[...] = x1 + _dot(o, wxo_ref[...])


def _mix_xattn(x2, ya, hf, hb, zo, mlstm_norm, wom, wol, xattn_norm, wxq, kx, vx, wxo, tm, S):
    T, D = x2.shape
    M = kx.shape[1]
    per_b = S // tm
    row = lambda n: pl.BlockSpec((tm, n), lambda i: (i, 0))
    mem_spec = pl.BlockSpec((1, M, D), lambda i: (i // per_b, 0, 0))
    x_scale = float((D // X_HEADS) ** -0.5)
    return pl.pallas_call(
        functools.partial(_mix_xattn_kernel, x_scale=x_scale),
        grid=(T // tm,),
        in_specs=[row(D), row(ya.shape[1]), row(hf.shape[1]), row(hb.shape[1]), row(zo.shape[1]),
                  _const_spec(mlstm_norm.shape), _const_spec(wom.shape), _const_spec(wol.shape),
                  _const_spec(xattn_norm.shape), _const_spec(wxq.shape), mem_spec, mem_spec,
                  _const_spec(wxo.shape)],
        out_specs=row(D),
        out_shape=jax.ShapeDtypeStruct((T, D), F32),
        compiler_params=_cparams(("arbitrary",)),
        name="mix_xattn",
    )(x2, ya, hf, hb, zo, mlstm_norm, wom, wol, xattn_norm, wxq, kx, vx, wxo)


def _ffn_kernel(x_ref, fn_ref, wg_ref, wu_ref, wd_ref, on_ref, o_ref, *, fc):
    x = x_ref[...]
    hn = _rms(x, fn_ref[...]).astype(BF16)
    dff = wg_ref.shape[1]
    acc = x
    for j in range(dff // fc):
        sl = slice(j * fc, (j + 1) * fc)
        g = _dot(hn, wg_ref[:, sl])
        u = _dot(hn, wu_ref[:, sl])
        acc = acc + _dot((g * jax.nn.sigmoid(g) * u).astype(BF16), wd_ref[sl, :])
    o_ref[...] = _rms(acc, on_ref[...])


def _ffn(x2, ffn_norm, wg, wu, wd, final_norm, tm, fc):
    T, D = x2.shape
    row = pl.BlockSpec((tm, D), lambda i: (i, 0))
    return pl.pallas_call(
        functools.partial(_ffn_kernel, fc=fc),
        grid=(T // tm,),
        in_specs=[row, _const_spec(ffn_norm.shape), _const_spec(wg.shape), _const_spec(wu.shape),
                  _const_spec(wd.shape), _const_spec(final_norm.shape)],
        out_specs=row,
        out_shape=jax.ShapeDtypeStruct((T, D), F32),
        compiler_params=_cparams(("arbitrary",)),
        name="ffn",
    )(x2, ffn_norm, wg, wu, wd, final_norm)


def _head_blocks(wmat, per_head, take, put):
    rows = wmat.shape[0]
    w3 = wmat.reshape(rows, MLA_HEADS, per_head)[:, :, take[0]:take[1]]
    out = jnp.zeros((rows, MLA_HEADS, LANES), wmat.dtype)
    out = out.at[:, :, put:put + (take[1] - take[0])].set(w3)
    return out.reshape(rows, MLA_HEADS * LANES)


def _lane_block(wmat, put):
    rows, n = wmat.shape
    return jnp.zeros((rows, LANES), wmat.dtype).at[:, put:put + n].set(wmat)


def kernel(x, mem, positions, attn_norm, w_in, q_norm, w_uq, kv_norm, w_ukv, mlstm_conv, mlstm_gate_bias,
           mlstm_norm, w_out, xattn_norm, mem_norm, w_xq, w_xkv, w_xo, ffn_norm, w_gate_up, w_down,
           final_norm):
    B, S, D = x.shape
    T = B * S
    assert w_in.shape[0] == 1, "single-layer problem: the final norm is fused into the SwiGLU kernel"
    l = 0
    tm = 512
    x2 = x.reshape(T, D)

    inv = ROPE_THETA ** (-jnp.arange(0, ROPE_DIM, 2, dtype=F32) / ROPE_DIM)
    ang = positions.astype(F32).reshape(T, 1) * inv
    cos, sin = jnp.cos(ang), jnp.sin(ang)
    pad = jnp.zeros((T, LANES - NOPE_DIM - ROPE_DIM), F32)
    ctab = jnp.concatenate([jnp.ones((T, NOPE_DIM), F32), cos, cos, pad], axis=-1)
    stab = jnp.concatenate([jnp.zeros((T, NOPE_DIM), F32), -sin, sin, pad], axis=-1)
    vone = jnp.tile(jnp.zeros((1, LANES), F32).at[0, V_DIM].set(1.0), (1, MLA_HEADS))

    o_cq, o_ckv, o_kr = 0, Q_LORA, Q_LORA + KV_LORA
    o_qk = o_kr + ROPE_DIM
    o_v = o_qk + 2 * MLSTM_WIDTH
    o_o = o_v + MLSTM_WIDTH
    o_g = o_o + MLSTM_WIDTH
    nh = MLSTM_HEADS
    qd = NOPE_DIM + ROPE_DIM
    wi = w_in[l]
    kr1 = wi[:, o_kr:o_kr + HALF_ROPE]
    kr2 = wi[:, o_kr + HALF_ROPE:o_qk]
    kr_a = _lane_block(jnp.concatenate([kr1, kr2], axis=1), NOPE_DIM)
    kr_b = _lane_block(jnp.concatenate([kr2, kr1], axis=1), NOPE_DIM)
    wg = wi[:, o_g:]
    gate_perm = lambda m: jnp.concatenate(
        [m[:, 0:nh], m[:, 2 * nh:3 * nh], m[:, nh:2 * nh], m[:, 3 * nh:4 * nh]], axis=1)
    wa = jnp.concatenate([wi[:, o_cq:o_kr], kr_a, kr_b, _lane_block(gate_perm(wg), 0)], axis=1).astype(BF16)
    wb = wi[:, o_qk:o_g].astype(BF16)
    gbias = _lane_block(gate_perm(mlstm_gate_bias[l][None, :]), 0)
    wq = w_uq[l]
    wq_a = (_head_blocks(wq, qd, (0, NOPE_DIM), 0)
            + _head_blocks(wq, qd, (NOPE_DIM, qd), NOPE_DIM))
    wq_b = (_head_blocks(wq, qd, (NOPE_DIM + HALF_ROPE, qd), NOPE_DIM)
            + _head_blocks(wq, qd, (NOPE_DIM, NOPE_DIM + HALF_ROPE), NOPE_DIM + HALF_ROPE))
    wqq = jnp.concatenate([wq_a, wq_b], axis=1).astype(BF16)
    wkv = w_ukv[l]
    kvd = NOPE_DIM + V_DIM
    wkv2 = jnp.concatenate([_head_blocks(wkv, kvd, (0, NOPE_DIM), 0),
                            _head_blocks(wkv, kvd, (NOPE_DIM, kvd), 0)], axis=1).astype(BF16)
    wo = w_out[l]
    wom = wo[:MLA_HEADS * V_DIM].astype(BF16)
    wol = wo[MLA_HEADS * V_DIM:].astype(BF16)
    dff = w_down.shape[1]

    kx, vx = _mem_kv(mem, mem_norm[l][None, :], w_xkv[l].astype(BF16))
    q, k, v, qm, kt, zv, zo, gc, gr = _in_proj(
        x2, ctab, stab, attn_norm[l][None, :], wa, wb, q_norm[l][None, :], wqq, kv_norm[l][None, :],
        wkv2, gbias, vone, mlstm_conv[l], tm, S)
    hw = MLA_HEADS * LANES
    ya = _mla_attention(q.reshape(B, S, hw), k.reshape(B, S, hw), v.reshape(B, S, hw), 512, 512, 4)
    hf, hb = _mlstm(qm.reshape(B, S, -1), kt, zv.reshape(B, S, -1), gc.reshape(B, S, -1), gr)
    x2 = _mix_xattn(x2, ya.reshape(T, -1), hf.reshape(T, -1), hb.reshape(T, -1), zo,
                    mlstm_norm[l][None, :], wom, wol, xattn_norm[l][None, :], w_xq[l].astype(BF16),
                    kx, vx, w_xo[l].astype(BF16), tm, S)
    wgu = w_gate_up[l].astype(BF16)
    y = _ffn(x2, ffn_norm[l][None, :], wgu[:, :dff], wgu[:, dff:], w_down[l].astype(BF16),
             final_norm[None, :], tm, 256)
    return y.reshape(B, S, D)
```

```python
import functools

import numpy as np
import jax
import jax.numpy as jnp
from jax import lax
from jax.experimental import pallas as pl
from jax.experimental.pallas import tpu as pltpu

F32 = jnp.float32
BF16 = jnp.bfloat16

EPS = 1e-6
MLA_HEADS = 8
Q_LORA = 256
KV_LORA = 128
NOPE_DIM = 64
ROPE_DIM = 32
V_DIM = 64
ROPE_THETA = 10000.0
MLSTM_HEADS = 4
MLSTM_HEAD_DIM = 128
MLSTM_WIDTH = MLSTM_HEADS * MLSTM_HEAD_DIM
MLSTM_CHUNK = 128
X_HEADS = 4
LANES = 128
HALF_ROPE = ROPE_DIM // 2

VMEM_LIMIT = 56 * 1024 * 1024


def _cparams(sem):
    return pltpu.CompilerParams(dimension_semantics=sem, vmem_limit_bytes=VMEM_LIMIT)


def _rms(x, g):
    ms = jnp.mean(x * x, axis=-1, keepdims=True)
    return x * lax.rsqrt(ms + EPS) * g


def _dot(a, b):
    return jnp.dot(a, b, preferred_element_type=F32)


def _dot_nt(a, b):
    return lax.dot_general(a, b, (((1,), (1,)), ((), ())), preferred_element_type=F32)


def _dot_tn(a, b):
    return lax.dot_general(a, b, (((0,), (0,)), ((), ())), preferred_element_type=F32)


def _const_spec(shape):
    return pl.BlockSpec(shape, lambda *_: (0,) * len(shape))


def _mem_kv_kernel(mem_ref, g_ref, w_ref, k_ref, v_ref):
    d = mem_ref.shape[-1]
    mn = _rms(mem_ref[0], g_ref[...]).astype(BF16)
    kv = _dot(mn, w_ref[...])
    k_ref[0] = kv[:, :d].astype(BF16)
    v_ref[0] = kv[:, d:].astype(BF16)


def _mem_kv(mem, mem_norm, w_xkv):
    B, M, D = mem.shape
    return pl.pallas_call(
        _mem_kv_kernel,
        grid=(B,),
        in_specs=[pl.BlockSpec((1, M, D), lambda b: (b, 0, 0)),
                  _const_spec((1, D)), _const_spec((D, 2 * D))],
        out_specs=[pl.BlockSpec((1, M, D), lambda b: (b, 0, 0))] * 2,
        out_shape=[jax.ShapeDtypeStruct((B, M, D), BF16)] * 2,
        compiler_params=_cparams(("arbitrary",)),
        name="mem_kv",
    )(mem, mem_norm, w_xkv)


HALO = 16
N_DIRS = 2
N_GATE_COLS = N_DIRS * MLSTM_HEADS


def _log_sigmoid(x):
    return jnp.minimum(x, 0.0) - jnp.log(1.0 + jnp.exp(-jnp.abs(x)))


def _chunk_scan(x, op, ident):
    n = x.shape[1]
    rows = lax.broadcasted_iota(jnp.int32, x.shape, 0)
    lanes = lax.broadcasted_iota(jnp.int32, x.shape, 1)
    fwd = rows < MLSTM_HEADS
    k = 1
    while k < n:
        down = pltpu.roll(x, k, axis=1)
        up = pltpu.roll(x, n - k, axis=1)
        shifted = jnp.where(fwd, jnp.where(lanes >= k, down, ident), jnp.where(lanes < n - k, up, ident))
        x = op(x, shifted)
        k *= 2
    return x


def _inproj_kernel(x_ref, xp_ref, xn_ref, ct_ref, st_ref, an_ref, wa_ref, wb_ref, qn_ref, wq_ref, kvn_ref,
                   wkv_ref, gb_ref, vone_ref, cw_ref,
                   q_ref, k_ref, v_ref, qm_ref, kt_ref, zv_ref, zo_ref, gc_ref, gr_ref,
                   *, q_scale, k_scale, tiles_per_seq):
    hw = MLA_HEADS * LANES
    tm = x_ref.shape[0]
    L = MLSTM_CHUNK
    w = MLSTM_WIDTH
    i = pl.program_id(0)
    first = (i % tiles_per_seq) == 0
    last = (i % tiles_per_seq) == tiles_per_seq - 1
    x_ext = jnp.concatenate([xp_ref[...], x_ref[...], xn_ref[...]], axis=0)
    h_ext = _rms(x_ext, an_ref[...]).astype(BF16)
    h = h_ext[HALO:HALO + tm]
    za = _dot(h, wa_ref[...])
    cq = za[:, :Q_LORA]
    ckv = za[:, Q_LORA:Q_LORA + KV_LORA]
    kra = za[:, Q_LORA + KV_LORA:Q_LORA + KV_LORA + LANES]
    krb = za[:, Q_LORA + KV_LORA + LANES:Q_LORA + KV_LORA + 2 * LANES]

    gt = (za[:, Q_LORA + KV_LORA + 2 * LANES:] + gb_ref[...]).T
    ng = N_GATE_COLS
    pad = jnp.zeros((L - ng, L), F32)
    for c in range(tm // L):
        cs = slice(c * L, (c + 1) * L)
        ic = gt[0:ng, cs]
        logf = _log_sigmoid(gt[ng:2 * ng, cs])
        bc = _chunk_scan(logf, jnp.add, 0.0)
        uc = ic - bc
        cmc = _chunk_scan(uc, jnp.maximum, -jnp.inf)
        ec = jnp.exp(uc - jnp.max(uc, axis=1, keepdims=True))
        gr_ref[0:ng, cs] = uc
        gr_ref[ng:2 * ng, cs] = ec
        gc_ref[cs, 0:LANES] = jnp.concatenate([cmc, pad], axis=0).T
        gc_ref[cs, LANES:2 * LANES] = jnp.concatenate([bc, pad], axis=0).T

    zqk = _dot(h_ext, wb_ref[:, :2 * w])
    zc = zqk[HALO:HALO + tm]
    z_all = jnp.concatenate([jnp.where(first, 0.0, zqk[:HALO]), zc, jnp.where(last, 0.0, zqk[HALO + tm:])],
                            axis=0)
    n_ext = tm + 2 * HALO
    zp = pltpu.roll(z_all, 1, axis=0)[HALO:HALO + tm]
    zn = pltpu.roll(z_all, n_ext - 1, axis=0)[HALO:HALO + tm]
    cw = cw_ref[...]
    y = zp * cw[0:1, :] + zc * cw[1:2, :]
    y = y + zn * cw[2:3, :]
    qk = y * jax.nn.sigmoid(y)
    qm_ref[...] = qk[:, :w].astype(BF16)
    kt_ref[...] = (qk[:, w:] * k_scale).T.astype(BF16)

    ct = ct_ref[...]
    st = st_ref[...]
    cqn = _rms(cq, qn_ref[...]).astype(BF16)
    ckvn = _rms(ckv, kvn_ref[...]).astype(BF16)
    qq = _dot(cqn, wq_ref[...])
    q = qq[:, :hw] * jnp.tile(ct, (1, MLA_HEADS)) + qq[:, hw:] * jnp.tile(st, (1, MLA_HEADS))
    q_ref[...] = (q * q_scale).astype(BF16)
    kv = _dot(ckvn, wkv_ref[...])
    krot = kra * ct + krb * st
    k_ref[...] = (kv[:, :hw] + jnp.tile(krot, (1, MLA_HEADS))).astype(BF16)
    v_ref[...] = (kv[:, hw:] + vone_ref[...]).astype(BF16)

    zvo = _dot(h, wb_ref[:, 2 * w:])
    zv_ref[...] = zvo[:, :w].astype(BF16)
    zo_ref[...] = zvo[:, w:].astype(BF16)


def _in_proj(x2, ctab, stab, attn_norm, wa, wb, q_norm, wq, kv_norm, wkv, gbias, vone, conv_w, tm, S):
    T, D = x2.shape
    hw = MLA_HEADS * LANES
    w = MLSTM_WIDTH
    per = tm // HALO
    nhb = T // HALO
    row = lambda n: pl.BlockSpec((tm, n), lambda i: (i, 0))
    col = lambda n: pl.BlockSpec((n, tm), lambda i: (0, i))
    prev = pl.BlockSpec((HALO, D), lambda i: (jnp.maximum(i * per - 1, 0), 0))
    nxt = pl.BlockSpec((HALO, D), lambda i: (jnp.minimum((i + 1) * per, nhb - 1), 0))
    out_specs = [row(hw), row(hw), row(hw), row(w), col(w), row(w), row(w), row(2 * LANES),
                 col(2 * N_GATE_COLS)]
    out_shape = [jax.ShapeDtypeStruct((T, hw), BF16)] * 3 + [
        jax.ShapeDtypeStruct((T, w), BF16), jax.ShapeDtypeStruct((w, T), BF16),
        jax.ShapeDtypeStruct((T, w), BF16), jax.ShapeDtypeStruct((T, w), BF16),
        jax.ShapeDtypeStruct((T, 2 * LANES), F32), jax.ShapeDtypeStruct((2 * N_GATE_COLS, T), F32)]
    q_scale = float((NOPE_DIM + ROPE_DIM) ** -0.5 * np.log2(np.e))
    return pl.pallas_call(
        functools.partial(_inproj_kernel, q_scale=q_scale, k_scale=float(MLSTM_HEAD_DIM ** -0.5),
                          tiles_per_seq=S // tm),
        grid=(T // tm,),
        in_specs=[row(D), prev, nxt, row(LANES), row(LANES), _const_spec(attn_norm.shape),
                  _const_spec(wa.shape), _const_spec(wb.shape), _const_spec(q_norm.shape),
                  _const_spec(wq.shape), _const_spec(kv_norm.shape), _const_spec(wkv.shape),
                  _const_spec(gbias.shape), _const_spec(vone.shape), _const_spec(conv_w.shape)],
        out_specs=out_specs,
        out_shape=out_shape,
        compiler_params=_cparams(("arbitrary",)),
        name="in_proj",
    )(x2, x2, x2, ctab, stab, attn_norm, wa, wb, q_norm, wq, kv_norm, wkv, gbias, vone, conv_w)


HEADS_PER_STEP = 2
V_ROWS = 80
Q_SUB = 256


def _mla_attn_kernel(q_ref, qn_ref, k_ref, v_ref, o_ref, vt_ref, qt_ref, s_ref, mc_ref, acc_ref, m_ref,
                     *, tk, steps_per_iter):
    S = k_ref.shape[1]
    tq = q_ref.shape[1]
    nk = S // tk
    assert steps_per_iter % 2 == 0 and nk % steps_per_iter == 0
    heads = range(HEADS_PER_STEP)
    qi = pl.program_id(2)
    par = qi % 2

    def load_qt(src_ref, qslot):
        for h in heads:
            qt_ref[qslot, h] = src_ref[0, :, h * LANES:(h + 1) * LANES].astype(F32).T.astype(BF16)

    nsub = tq // Q_SUB
    units = [(h, slice(i * Q_SUB, (i + 1) * Q_SUB)) for h in heads for i in range(nsub)]

    def scores(off, qslot, slot, h, qs):
        st = _dot(k_ref[0, pl.ds(off, tk), h * LANES:(h + 1) * LANES], qt_ref[qslot, h, :, qs])
        s_ref[slot, h, :, qs] = st
        mc_ref[slot, h, :, qs] = jnp.max(st, axis=0, keepdims=True)

    def consume(off, slot, h, qs):
        m_old = m_ref[h, :, qs]
        m_new = jnp.maximum(m_old, mc_ref[slot, h, :, qs])
        alpha = jnp.exp2(m_old - m_new)
        p = jnp.exp2(s_ref[slot, h, :, qs] - m_new).astype(BF16)
        acc_ref[h, :, qs] = alpha * acc_ref[h, :, qs] + _dot(vt_ref[h, 0:V_ROWS, pl.ds(off, tk)], p)
        m_ref[h, :, qs] = m_new

    @pl.when(qi == 0)
    def _():
        for h in heads:
            for c in range(nk):
                blk = v_ref[0, c * tk:(c + 1) * tk, h * LANES:(h + 1) * LANES].astype(F32)
                vt_ref[h, :, c * tk:(c + 1) * tk] = blk.T.astype(BF16)
        load_qt(q_ref, 0)
        for h, qs in units:
            scores(0, 0, 0, h, qs)

    load_qt(qn_ref, 1 - par)
    for h in heads:
        m_ref[h] = jnp.full((1, tq), -jnp.inf, F32)
        acc_ref[h] = jnp.zeros((V_ROWS, tq), F32)

    def step(c, slot):
        wrap = c + 1 == nk
        nxt = jnp.where(wrap, 0, c + 1)
        for h, qs in units:
            scores(pl.multiple_of(nxt * tk, tk), jnp.where(wrap, 1 - par, par), 1 - slot, h, qs)
            consume(pl.multiple_of(c * tk, tk), slot, h, qs)

    def body(jj, carry):
        for u in range(steps_per_iter):
            step(steps_per_iter * jj + u, u % 2)
        return carry

    lax.fori_loop(0, nk // steps_per_iter, body, 0)
    outs = []
    for h in heads:
        acc = acc_ref[h]
        o_t = acc[0:V_DIM, :] * (1.0 / acc[V_DIM:V_DIM + 1, :])
        outs.append(o_t.T)
    o_ref[0] = jnp.concatenate(outs, axis=-1).astype(BF16)


def _mla_attention(q, k, v, tq, tk, steps_per_iter):
    B, S, _ = q.shape
    hp = HEADS_PER_STEP
    nq = S // tq
    return pl.pallas_call(
        functools.partial(_mla_attn_kernel, tk=tk, steps_per_iter=steps_per_iter),
        grid=(B, MLA_HEADS // hp, nq),
        in_specs=[pl.BlockSpec((1, tq, hp * LANES), lambda b, h, i: (b, i, h)),
                  pl.BlockSpec((1, tq, hp * LANES), lambda b, h, i: (b, jnp.minimum(i + 1, nq - 1), h)),
                  pl.BlockSpec((1, S, hp * LANES), lambda b, h, i: (b, 0, h)),
                  pl.BlockSpec((1, S, hp * LANES), lambda b, h, i: (b, 0, h))],
        out_specs=pl.BlockSpec((1, tq, hp * V_DIM), lambda b, h, i: (b, i, h)),
        out_shape=jax.ShapeDtypeStruct((B, S, MLA_HEADS * V_DIM), BF16),
        scratch_shapes=[pltpu.VMEM((hp, LANES, S), BF16),
                        pltpu.VMEM((2, hp, LANES, tq), BF16),
                        pltpu.VMEM((2, hp, tk, tq), F32),
                        pltpu.VMEM((2, hp, 1, tq), F32),
                        pltpu.VMEM((hp, V_ROWS, tq), F32),
                        pltpu.VMEM((hp, 1, tq), F32)],
        compiler_params=_cparams(("arbitrary", "arbitrary", "arbitrary")),
        name="mla_attn",
    )(q, q, k, v)


def _lane_bcast(x, j):
    return jnp.broadcast_to(x[:, j:j + 1], x.shape)


def _mlstm_kernel(qf_ref, ktf_ref, vf_ref, gcf_ref, grf_ref, qb_ref, ktb_ref, vb_ref, gcb_ref, grb_ref,
                  hf_ref, hb_ref, ct_ref, m_ref):
    @pl.when(pl.program_id(1) == 0)
    def _():
        ct_ref[...] = jnp.zeros_like(ct_ref)
        m_ref[...] = jnp.zeros_like(m_ref)

    L = MLSTM_CHUNK
    dh = MLSTM_HEAD_DIM
    nh = MLSTM_HEADS
    rows = lax.broadcasted_iota(jnp.int32, (L, L), 0)
    cols = lax.broadcasted_iota(jnp.int32, (L, L), 1)
    m_row = m_ref[...]
    dirs = ((qf_ref, ktf_ref, vf_ref, gcf_ref, grf_ref, hf_ref, False),
            (qb_ref, ktb_ref, vb_ref, gcb_ref, grb_ref, hb_ref, True))

    gate = []
    for d, (_, _, _, gc_ref, gr_ref, _, rev) in enumerate(dirs):
        gc = gc_ref[0]
        cm = gc[:, :LANES]
        b = gc[:, LANES:]
        g = jnp.maximum(m_row, cm)
        floor = jnp.exp(-(b + g))
        end = 0 if rev else L - 1
        blast = b[end:end + 1, :]
        m_loc = blast + cm[end:end + 1, :]
        m_new = jnp.maximum(blast + m_row, m_loc)
        a_row = jnp.exp(blast + m_row - m_new)
        c_row = jnp.exp(m_loc - m_new)
        mask = (cols >= rows) if rev else (cols <= rows)
        gate.append((g, floor, a_row, c_row, m_new, mask, gr_ref[...]))
    lane1 = lax.broadcasted_iota(jnp.int32, m_row.shape, 1)
    m_ref[...] = jnp.where(lane1 < nh, gate[0][4], gate[1][4])

    units = [(d, h) for d in range(N_DIRS) for h in range(nh)]
    q_l, kt_l, vx_l, sc_l, qc_l = [], [], [], [], []
    for d, h in units:
        j = d * nh + h
        hs = slice(h * dh, (h + 1) * dh)
        q = dirs[d][0][0, :, hs]
        kt = dirs[d][1][hs, :]
        v_ext = jnp.concatenate([dirs[d][2][0, :, hs], jnp.where(cols == j, 1.0, 0.0).astype(BF16)], axis=1)
        q_l.append(q)
        kt_l.append(kt)
        vx_l.append(v_ext)
        sc_l.append(_dot(q, kt))
        qc_l.append(_dot(q, ct_ref[j].astype(BF16)))
    s_l, kte_l, a_l = [], [], []
    for u, (d, h) in enumerate(units):
        j = d * nh + h
        g, _, _, _, _, mask, gr = gate[d]
        g_rep = _lane_bcast(g, j)
        a_l.append(jnp.exp(m_row[:, j:j + 1] - g_rep))
        p = jnp.exp(jnp.where(mask, gr[j:j + 1, :] - g_rep, -jnp.inf))
        s_l.append((sc_l[u] * p).astype(BF16))
        kte_l.append((kt_l[u].astype(F32) * gr[N_GATE_COLS + j:N_GATE_COLS + j + 1, :]).astype(BF16))
    nd_l = [_dot(s_l[u], vx_l[u]) for u in range(len(units))]
    cl_l = [_dot(kte_l[u], vx_l[u]) for u in range(len(units))]
    for u, (d, h) in enumerate(units):
        j = d * nh + h
        _, floor, a_row, c_row, _, _, _ = gate[d]
        tot = nd_l[u] + jnp.concatenate([a_l[u], a_l[u]], axis=1) * qc_l[u]
        r = 1.0 / jnp.maximum(jnp.abs(tot[:, LANES:]), floor)
        dirs[d][5][0, :, h * dh:(h + 1) * dh] = tot[:, :LANES] * _lane_bcast(r, j)
        ct_ref[j] = a_row[:, j:j + 1] * ct_ref[j] + c_row[:, j:j + 1] * cl_l[u]


def _mlstm(qm, kt, zv, gc, gr):
    B, S, w = qm.shape
    L = MLSTM_CHUNK
    nc = S // L
    dh = MLSTM_HEAD_DIM
    fwd = lambda b, c: (b, c, 0)
    bwd = lambda b, c: (b, nc - 1 - c, 0)
    fwd_t = lambda b, c: (0, b * nc + c)
    bwd_t = lambda b, c: (0, b * nc + nc - 1 - c)
    ng2 = 2 * N_GATE_COLS

    def specs(im, im_t):
        return [pl.BlockSpec((1, L, w), im), pl.BlockSpec((w, L), im_t), pl.BlockSpec((1, L, w), im),
                pl.BlockSpec((1, L, 2 * LANES), im), pl.BlockSpec((ng2, L), im_t)]

    return pl.pallas_call(
        _mlstm_kernel,
        grid=(B, nc),
        in_specs=specs(fwd, fwd_t) + specs(bwd, bwd_t),
        out_specs=[pl.BlockSpec((1, L, w), fwd), pl.BlockSpec((1, L, w), bwd)],
        out_shape=[jax.ShapeDtypeStruct((B, S, w), F32)] * 2,
        scratch_shapes=[pltpu.VMEM((N_GATE_COLS, dh, 2 * LANES), F32),
                        pltpu.VMEM((1, LANES), F32)],
        compiler_params=_cparams(("arbitrary", "arbitrary")),
        name="mlstm",
    )(qm, kt, zv, gc, gr, qm, kt, zv, gc, gr)


def _mix_xattn_kernel(x_ref, ya_ref, hf_ref, hb_ref, zo_ref, mn_ref, wom_ref, wol_ref, xn_ref, wxq_ref,
                      kx_ref, vx_ref, wxo_ref, o_ref, *, x_scale):
    dh = MLSTM_HEAD_DIM
    hs = hf_ref[...] + hb_ref[...]
    parts = []
    for h in range(MLSTM_HEADS):
        sl = slice(h * dh, (h + 1) * dh)
        parts.append(_rms(hs[:, sl], mn_ref[:, sl]))
    hn = jnp.concatenate(parts, axis=-1)
    yl = (jax.nn.sigmoid(zo_ref[...].astype(F32)) * hn).astype(BF16)
    x1 = x_ref[...] + _dot(ya_ref[...], wom_ref[...]) + _dot(yl, wol_ref[...])
    hq = _rms(x1, xn_ref[...]).astype(BF16)
    q = (_dot(hq, wxq_ref[...]) * x_scale).astype(BF16)
    xd = q.shape[-1] // X_HEADS
    sls = [slice(h * xd, (h + 1) * xd) for h in range(X_HEADS)]
    ss = [_dot_nt(q[:, sl], kx_ref[0, :, sl]) for sl in sls]
    ps = [jnp.exp2(s - jnp.max(s, axis=-1, keepdims=True)) for s in ss]
    ls = [jnp.sum(p, axis=-1, keepdims=True) for p in ps]
    pv = [_dot(p.astype(BF16), vx_ref[0, :, sl]) for p, sl in zip(ps, sls)]
    o = jnp.concatenate([(a * (1.0 / l)).astype(BF16) for a, l in zip(pv, ls)], axis=-1)
    o_ref[...] = x1 + _dot(o, wxo_ref[...])


def _mix_xattn(x2, ya, hf, hb, zo, mlstm_norm, wom, wol, xattn_norm, wxq, kx, vx, wxo, tm, S):
    T, D = x2.shape
    M = kx.shape[1]
    per_b = S // tm
    row = lambda n: pl.BlockSpec((tm, n), lambda i: (i, 0))
    mem_spec = pl.BlockSpec((1, M, D), lambda i: (i // per_b, 0, 0))
    x_scale = float((D // X_HEADS) ** -0.5 * np.log2(np.e))
    return pl.pallas_call(
        functools.partial(_mix_xattn_kernel, x_scale=x_scale),
        grid=(T // tm,),
        in_specs=[row(D), row(ya.shape[1]), row(hf.shape[1]), row(hb.shape[1]), row(zo.shape[1]),
                  _const_spec(mlstm_norm.shape), _const_spec(wom.shape), _const_spec(wol.shape),
                  _const_spec(xattn_norm.shape), _const_spec(wxq.shape), mem_spec, mem_spec,
                  _const_spec(wxo.shape)],
        out_specs=row(D),
        out_shape=jax.ShapeDtypeStruct((T, D), F32),
        compiler_params=_cparams(("arbitrary",)),
        name="mix_xattn",
    )(x2, ya, hf, hb, zo, mlstm_norm, wom, wol, xattn_norm, wxq, kx, vx, wxo)


def _ffn_kernel(x_ref, fn_ref, wg_ref, wu_ref, wd_ref, on_ref, o_ref, *, fc):
    x = x_ref[...]
    hn = _rms(x, fn_ref[...]).astype(BF16)
    dff = wg_ref.shape[1]
    acc = x
    for j in range(dff // fc):
        sl = slice(j * fc, (j + 1) * fc)
        g = _dot(hn, wg_ref[:, sl])
        u = _dot(hn, wu_ref[:, sl])
        acc = acc + _dot((g * jax.nn.sigmoid(g) * u).astype(BF16), wd_ref[sl, :])
    o_ref[...] = _rms(acc, on_ref[...])


def _ffn(x2, ffn_norm, wg, wu, wd, final_norm, tm, fc):
    T, D = x2.shape
    row = pl.BlockSpec((tm, D), lambda i: (i, 0))
    return pl.pallas_call(
        functools.partial(_ffn_kernel, fc=fc),
        grid=(T // tm,),
        in_specs=[row, _const_spec(ffn_norm.shape), _const_spec(wg.shape), _const_spec(wu.shape),
                  _const_spec(wd.shape), _const_spec(final_norm.shape)],
        out_specs=row,
        out_shape=jax.ShapeDtypeStruct((T, D), F32),
        compiler_params=_cparams(("arbitrary",)),
        name="ffn",
    )(x2, ffn_norm, wg, wu, wd, final_norm)


def _head_blocks(wmat, per_head, take, put):
    rows = wmat.shape[0]
    w3 = wmat.reshape(rows, MLA_HEADS, per_head)[:, :, take[0]:take[1]]
    out = jnp.zeros((rows, MLA_HEADS, LANES), wmat.dtype)
    out = out.at[:, :, put:put + (take[1] - take[0])].set(w3)
    return out.reshape(rows, MLA_HEADS * LANES)


def _lane_block(wmat, put):
    rows, n = wmat.shape
    return jnp.zeros((rows, LANES), wmat.dtype).at[:, put:put + n].set(wmat)


def kernel(x, mem, positions, attn_norm, w_in, q_norm, w_uq, kv_norm, w_ukv, mlstm_conv, mlstm_gate_bias,
           mlstm_norm, w_out, xattn_norm, mem_norm, w_xq, w_xkv, w_xo, ffn_norm, w_gate_up, w_down,
           final_norm):
    B, S, D = x.shape
    T = B * S
    assert w_in.shape[0] == 1, "single-layer problem: the final norm is fused into the SwiGLU kernel"
    l = 0
    tm = 512
    x2 = x.reshape(T, D)

    inv = ROPE_THETA ** (-jnp.arange(0, ROPE_DIM, 2, dtype=F32) / ROPE_DIM)
    ang = positions.astype(F32).reshape(T, 1) * inv
    cos, sin = jnp.cos(ang), jnp.sin(ang)
    pad = jnp.zeros((T, LANES - NOPE_DIM - ROPE_DIM), F32)
    ctab = jnp.concatenate([jnp.ones((T, NOPE_DIM), F32), cos, cos, pad], axis=-1)
    stab = jnp.concatenate([jnp.zeros((T, NOPE_DIM), F32), -sin, sin, pad], axis=-1)
    vone = jnp.tile(jnp.zeros((1, LANES), F32).at[0, V_DIM].set(1.0), (1, MLA_HEADS))

    o_cq, o_ckv, o_kr = 0, Q_LORA, Q_LORA + KV_LORA
    o_qk = o_kr + ROPE_DIM
    o_v = o_qk + 2 * MLSTM_WIDTH
    o_o = o_v + MLSTM_WIDTH
    o_g = o_o + MLSTM_WIDTH
    nh = MLSTM_HEADS
    qd = NOPE_DIM + ROPE_DIM
    wi = w_in[l]
    kr1 = wi[:, o_kr:o_kr + HALF_ROPE]
    kr2 = wi[:, o_kr + HALF_ROPE:o_qk]
    kr_a = _lane_block(jnp.concatenate([kr1, kr2], axis=1), NOPE_DIM)
    kr_b = _lane_block(jnp.concatenate([kr2, kr1], axis=1), NOPE_DIM)
    wg = wi[:, o_g:]
    gate_perm = lambda m: jnp.concatenate(
        [m[:, 0:nh], m[:, 2 * nh:3 * nh], m[:, nh:2 * nh], m[:, 3 * nh:4 * nh]], axis=1)
    wa = jnp.concatenate([wi[:, o_cq:o_kr], kr_a, kr_b, _lane_block(gate_perm(wg), 0)], axis=1).astype(BF16)
    wb = wi[:, o_qk:o_g].astype(BF16)
    gbias = _lane_block(gate_perm(mlstm_gate_bias[l][None, :]), 0)
    wq = w_uq[l]
    wq_a = (_head_blocks(wq, qd, (0, NOPE_DIM), 0)
            + _head_blocks(wq, qd, (NOPE_DIM, qd), NOPE_DIM))
    wq_b = (_head_blocks(wq, qd, (NOPE_DIM + HALF_ROPE, qd), NOPE_DIM)
            + _head_blocks(wq, qd, (NOPE_DIM, NOPE_DIM + HALF_ROPE), NOPE_DIM + HALF_ROPE))
    wqq = jnp.concatenate([wq_a, wq_b], axis=1).astype(BF16)
    wkv = w_ukv[l]
    kvd = NOPE_DIM + V_DIM
    wkv2 = jnp.concatenate([_head_blocks(wkv, kvd, (0, NOPE_DIM), 0),
                            _head_blocks(wkv, kvd, (NOPE_DIM, kvd), 0)], axis=1).astype(BF16)
    wo = w_out[l]
    wom = wo[:MLA_HEADS * V_DIM].astype(BF16)
    wol = wo[MLA_HEADS * V_DIM:].astype(BF16)
    dff = w_down.shape[1]

    kx, vx = _mem_kv(mem, mem_norm[l][None, :], w_xkv[l].astype(BF16))
    q, k, v, qm, kt, zv, zo, gc, gr = _in_proj(
        x2, ctab, stab, attn_norm[l][None, :], wa, wb, q_norm[l][None, :], wqq, kv_norm[l][None, :],
        wkv2, gbias, vone, mlstm_conv[l], tm, S)
    hw = MLA_HEADS * LANES
    ya = _mla_attention(q.reshape(B, S, hw), k.reshape(B, S, hw), v.reshape(B, S, hw), 2048, 512, 4)
    hf, hb = _mlstm(qm.reshape(B, S, -1), kt, zv.reshape(B, S, -1), gc.reshape(B, S, -1), gr)
    x2 = _mix_xattn(x2, ya.reshape(T, -1), hf.reshape(T, -1), hb.reshape(T, -1), zo,
                    mlstm_norm[l][None, :], wom, wol, xattn_norm[l][None, :], w_xq[l].astype(BF16),
                    kx, vx, w_xo[l].astype(BF16), tm, S)
    wgu = w_gate_up[l].astype(BF16)
    y = _ffn(x2, ffn_norm[l][None, :], wgu[:, :dff], wgu[:, dff:], w_down[l].astype(BF16),
             final_norm[None, :], tm, 256)
    return y.reshape(B, S, D)
```

```python
import functools

import numpy as np
import jax
import jax.numpy as jnp
from jax import lax
from jax.experimental import pallas as pl
from jax.experimental.pallas import tpu as pltpu

F32 = jnp.float32
BF16 = jnp.bfloat16

EPS = 1e-6
MLA_HEADS = 8
Q_LORA = 256
KV_LORA = 128
NOPE_DIM = 64
ROPE_DIM = 32
V_DIM = 64
ROPE_THETA = 10000.0
MLSTM_HEADS = 4
MLSTM_HEAD_DIM = 128
MLSTM_WIDTH = MLSTM_HEADS * MLSTM_HEAD_DIM
MLSTM_CHUNK = 128
X_HEADS = 4
LANES = 128
HALF_ROPE = ROPE_DIM // 2

VMEM_LIMIT = 56 * 1024 * 1024


def _cparams(sem):
    return pltpu.CompilerParams(dimension_semantics=sem, vmem_limit_bytes=VMEM_LIMIT)


def _rms(x, g):
    ms = jnp.mean(x * x, axis=-1, keepdims=True)
    return x * lax.rsqrt(ms + EPS) * g


def _dot(a, b):
    return jnp.dot(a, b, preferred_element_type=F32)


def _dot_nt(a, b):
    return lax.dot_general(a, b, (((1,), (1,)), ((), ())), preferred_element_type=F32)


def _dot_tn(a, b):
    return lax.dot_general(a, b, (((0,), (0,)), ((), ())), preferred_element_type=F32)


def _const_spec(shape):
    return pl.BlockSpec(shape, lambda *_: (0,) * len(shape))


def _mem_kv_kernel(mem_ref, g_ref, w_ref, k_ref, v_ref):
    d = mem_ref.shape[-1]
    mn = _rms(mem_ref[0], g_ref[...]).astype(BF16)
    kv = _dot(mn, w_ref[...])
    k_ref[0] = kv[:, :d].astype(BF16)
    v_ref[0] = kv[:, d:].astype(BF16)


def _mem_kv(mem, mem_norm, w_xkv):
    B, M, D = mem.shape
    return pl.pallas_call(
        _mem_kv_kernel,
        grid=(B,),
        in_specs=[pl.BlockSpec((1, M, D), lambda b: (b, 0, 0)),
                  _const_spec((1, D)), _const_spec((D, 2 * D))],
        out_specs=[pl.BlockSpec((1, M, D), lambda b: (b, 0, 0))] * 2,
        out_shape=[jax.ShapeDtypeStruct((B, M, D), BF16)] * 2,
        compiler_params=_cparams(("arbitrary",)),
        name="mem_kv",
    )(mem, mem_norm, w_xkv)


def _rope_kernel(ang_ref, cos_ref, sin_ref):
    ang = ang_ref[...]
    cos_ref[...] = jnp.cos(ang)
    sin_ref[...] = jnp.sin(ang)


def _rope_tables(positions):
    T = positions.size
    inv = ROPE_THETA ** (-jnp.arange(0, ROPE_DIM, 2, dtype=F32) / ROPE_DIM)
    ang = (positions.astype(F32).reshape(T, 1) * inv).reshape(T * HALF_ROPE // LANES, LANES)
    rows = ang.shape[0]
    br = min(rows, 1024)
    spec = pl.BlockSpec((br, LANES), lambda i: (i, 0))
    cos, sin = pl.pallas_call(
        _rope_kernel,
        grid=(rows // br,),
        in_specs=[spec],
        out_specs=[spec, spec],
        out_shape=[jax.ShapeDtypeStruct(ang.shape, F32)] * 2,
        compiler_params=_cparams(("arbitrary",)),
        name="rope_tab",
    )(ang)
    return cos.reshape(T, HALF_ROPE), sin.reshape(T, HALF_ROPE)


HALO = 16
N_DIRS = 2
N_GATE_COLS = N_DIRS * MLSTM_HEADS


def _log_sigmoid(x):
    return jnp.minimum(x, 0.0) - jnp.log(1.0 + jnp.exp(-jnp.abs(x)))


def _chunk_scan(x, op, ident):
    n = x.shape[1]
    rows = lax.broadcasted_iota(jnp.int32, x.shape, 0)
    lanes = lax.broadcasted_iota(jnp.int32, x.shape, 1)
    fwd = rows < MLSTM_HEADS
    k = 1
    while k < n:
        down = pltpu.roll(x, k, axis=1)
        up = pltpu.roll(x, n - k, axis=1)
        shifted = jnp.where(fwd, jnp.where(lanes >= k, down, ident), jnp.where(lanes < n - k, up, ident))
        x = op(x, shifted)
        k *= 2
    return x


def _inproj_kernel(x_ref, xp_ref, xn_ref, cos_ref, sin_ref, an_ref, wa_ref, wb_ref, qn_ref, wq_ref, kvn_ref,
                   wkv_ref, gb_ref, vone_ref, cw_ref,
                   q_ref, k_ref, v_ref, qm_ref, kt_ref, zv_ref, zo_ref, gc_ref, gr_ref,
                   *, q_scale, k_scale, tiles_per_seq):
    hw = MLA_HEADS * LANES
    tm = x_ref.shape[0]
    L = MLSTM_CHUNK
    w = MLSTM_WIDTH
    i = pl.program_id(0)
    first = (i % tiles_per_seq) == 0
    last = (i % tiles_per_seq) == tiles_per_seq - 1
    x_ext = jnp.concatenate([xp_ref[...], x_ref[...], xn_ref[...]], axis=0)
    h_ext = _rms(x_ext, an_ref[...]).astype(BF16)
    h = h_ext[HALO:HALO + tm]
    za = _dot(h, wa_ref[...])
    cq = za[:, :Q_LORA]
    ckv = za[:, Q_LORA:Q_LORA + KV_LORA]
    kra = za[:, Q_LORA + KV_LORA:Q_LORA + KV_LORA + LANES]
    krb = za[:, Q_LORA + KV_LORA + LANES:Q_LORA + KV_LORA + 2 * LANES]

    gt = (za[:, Q_LORA + KV_LORA + 2 * LANES:] + gb_ref[...]).T
    ng = N_GATE_COLS
    pad = jnp.zeros((L - ng, L), F32)
    for c in range(tm // L):
        cs = slice(c * L, (c + 1) * L)
        ic = gt[0:ng, cs]
        logf = _log_sigmoid(gt[ng:2 * ng, cs])
        bc = _chunk_scan(logf, jnp.add, 0.0)
        uc = ic - bc
        cmc = _chunk_scan(uc, jnp.maximum, -jnp.inf)
        ec = jnp.exp(uc - jnp.max(uc, axis=1, keepdims=True))
        gr_ref[0:ng, cs] = uc
        gr_ref[ng:2 * ng, cs] = ec
        gc_ref[cs, 0:LANES] = jnp.concatenate([cmc, pad], axis=0).T
        gc_ref[cs, LANES:2 * LANES] = jnp.concatenate([bc, pad], axis=0).T

    n_ext = tm + 2 * HALO
    cw = cw_ref[...]

    def conv_silu(cols):
        z = _dot(h_ext, wb_ref[:, cols])
        zc = z[HALO:HALO + tm]
        z_all = jnp.concatenate([jnp.where(first, 0.0, z[:HALO]), zc, jnp.where(last, 0.0, z[HALO + tm:])],
                                axis=0)
        zp = pltpu.roll(z_all, 1, axis=0)[HALO:HALO + tm]
        zn = pltpu.roll(z_all, n_ext - 1, axis=0)[HALO:HALO + tm]
        y = zp * cw[0:1, cols] + zc * cw[1:2, cols]
        y = y + zn * cw[2:3, cols]
        return y * jax.nn.sigmoid(y)

    qm_ref[...] = conv_silu(slice(0, w)).astype(BF16)
    kt_ref[...] = (conv_silu(slice(w, 2 * w)) * k_scale).T.astype(BF16)

    cos = cos_ref[...]
    sin = sin_ref[...]
    tail = jnp.zeros((tm, LANES - NOPE_DIM - ROPE_DIM), F32)
    ct = jnp.concatenate([jnp.ones((tm, NOPE_DIM), F32), cos, cos, tail], axis=1)
    st = jnp.concatenate([jnp.zeros((tm, NOPE_DIM), F32), -sin, sin, tail], axis=1)
    cqn = _rms(cq, qn_ref[...]).astype(BF16)
    ckvn = _rms(ckv, kvn_ref[...]).astype(BF16)
    qq = _dot(cqn, wq_ref[...])
    q = qq[:, :hw] * jnp.tile(ct, (1, MLA_HEADS)) + qq[:, hw:] * jnp.tile(st, (1, MLA_HEADS))
    q_ref[...] = (q * q_scale).astype(BF16)
    kv = _dot(ckvn, wkv_ref[...])
    krot = kra * ct + krb * st
    k_ref[...] = (kv[:, :hw] + jnp.tile(krot, (1, MLA_HEADS))).astype(BF16)
    v_ref[...] = (kv[:, hw:] + vone_ref[...]).astype(BF16)

    zvo = _dot(h, wb_ref[:, 2 * w:])
    zv_ref[...] = zvo[:, :w].astype(BF16)
    zo_ref[...] = zvo[:, w:].astype(BF16)


def _in_proj(x2, cos, sin, attn_norm, wa, wb, q_norm, wq, kv_norm, wkv, gbias, vone, conv_w, tm, S):
    T, D = x2.shape
    hw = MLA_HEADS * LANES
    w = MLSTM_WIDTH
    per = tm // HALO
    nhb = T // HALO
    row = lambda n: pl.BlockSpec((tm, n), lambda i: (i, 0))
    col = lambda n: pl.BlockSpec((n, tm), lambda i: (0, i))
    prev = pl.BlockSpec((HALO, D), lambda i: (jnp.maximum(i * per - 1, 0), 0))
    nxt = pl.BlockSpec((HALO, D), lambda i: (jnp.minimum((i + 1) * per, nhb - 1), 0))
    out_specs = [row(hw), row(hw), row(hw), row(w), col(w), row(w), row(w), row(2 * LANES),
                 col(2 * N_GATE_COLS)]
    out_shape = [jax.ShapeDtypeStruct((T, hw), BF16)] * 3 + [
        jax.ShapeDtypeStruct((T, w), BF16), jax.ShapeDtypeStruct((w, T), BF16),
        jax.ShapeDtypeStruct((T, w), BF16), jax.ShapeDtypeStruct((T, w), BF16),
        jax.ShapeDtypeStruct((T, 2 * LANES), F32), jax.ShapeDtypeStruct((2 * N_GATE_COLS, T), F32)]
    q_scale = float((NOPE_DIM + ROPE_DIM) ** -0.5 * np.log2(np.e))
    return pl.pallas_call(
        functools.partial(_inproj_kernel, q_scale=q_scale, k_scale=float(MLSTM_HEAD_DIM ** -0.5),
                          tiles_per_seq=S // tm),
        grid=(T // tm,),
        in_specs=[row(D), prev, nxt, row(HALF_ROPE), row(HALF_ROPE), _const_spec(attn_norm.shape),
                  _const_spec(wa.shape), _const_spec(wb.shape), _const_spec(q_norm.shape),
                  _const_spec(wq.shape), _const_spec(kv_norm.shape), _const_spec(wkv.shape),
                  _const_spec(gbias.shape), _const_spec(vone.shape), _const_spec(conv_w.shape)],
        out_specs=out_specs,
        out_shape=out_shape,
        compiler_params=_cparams(("arbitrary",)),
        name="in_proj",
    )(x2, x2, x2, cos, sin, attn_norm, wa, wb, q_norm, wq, kv_norm, wkv, gbias, vone, conv_w)


HEADS_PER_STEP = 2
V_ROWS = 80
Q_SUB = 256


def _mla_attn_kernel(q_ref, qn_ref, k_ref, v_ref, o_ref, vt_ref, qt_ref, s_ref, mc_ref, acc_ref, m_ref,
                     *, tk, steps_per_iter):
    S = k_ref.shape[1]
    tq = q_ref.shape[1]
    nk = S // tk
    assert steps_per_iter % 2 == 0 and nk % steps_per_iter == 0
    heads = range(HEADS_PER_STEP)
    qi = pl.program_id(2)
    par = qi % 2

    def load_qt(src_ref, qslot):
        for h in heads:
            qt_ref[qslot, h] = src_ref[0, :, h * LANES:(h + 1) * LANES].astype(F32).T.astype(BF16)

    nsub = tq // Q_SUB
    units = [(h, slice(i * Q_SUB, (i + 1) * Q_SUB)) for h in heads for i in range(nsub)]

    def scores(off, qslot, slot, h, qs):
        st = _dot(k_ref[0, pl.ds(off, tk), h * LANES:(h + 1) * LANES], qt_ref[qslot, h, :, qs])
        s_ref[slot, h, :, qs] = st
        mc_ref[slot, h, :, qs] = jnp.max(st, axis=0, keepdims=True)

    def consume(off, slot, h, qs):
        m_old = m_ref[h, :, qs]
        m_new = jnp.maximum(m_old, mc_ref[slot, h, :, qs])
        alpha = jnp.exp2(m_old - m_new)
        p = jnp.exp2(s_ref[slot, h, :, qs] - m_new).astype(BF16)
        acc_ref[h, :, qs] = alpha * acc_ref[h, :, qs] + _dot(vt_ref[h, 0:V_ROWS, pl.ds(off, tk)], p)
        m_ref[h, :, qs] = m_new

    @pl.when(qi == 0)
    def _():
        for h in heads:
            for c in range(nk):
                blk = v_ref[0, c * tk:(c + 1) * tk, h * LANES:(h + 1) * LANES].astype(F32)
                vt_ref[h, :, c * tk:(c + 1) * tk] = blk.T.astype(BF16)
        load_qt(q_ref, 0)
        for h, qs in units:
            scores(0, 0, 0, h, qs)

    load_qt(qn_ref, 1 - par)
    for h in heads:
        m_ref[h] = jnp.full((1, tq), -jnp.inf, F32)
        acc_ref[h] = jnp.zeros((V_ROWS, tq), F32)

    def step(c, slot):
        wrap = c + 1 == nk
        nxt = jnp.where(wrap, 0, c + 1)
        for h, qs in units:
            scores(pl.multiple_of(nxt * tk, tk), jnp.where(wrap, 1 - par, par), 1 - slot, h, qs)
            consume(pl.multiple_of(c * tk, tk), slot, h, qs)

    def body(jj, carry):
        for u in range(steps_per_iter):
            step(steps_per_iter * jj + u, u % 2)
        return carry

    lax.fori_loop(0, nk // steps_per_iter, body, 0)
    outs = []
    for h in heads:
        acc = acc_ref[h]
        o_t = acc[0:V_DIM, :] * (1.0 / acc[V_DIM:V_DIM + 1, :])
        outs.append(o_t.T)
    o_ref[0] = jnp.concatenate(outs, axis=-1).astype(BF16)


def _mla_attention(q, k, v, tq, tk, steps_per_iter):
    B, S, _ = q.shape
    hp = HEADS_PER_STEP
    nq = S // tq
    return pl.pallas_call(
        functools.partial(_mla_attn_kernel, tk=tk, steps_per_iter=steps_per_iter),
        grid=(B, MLA_HEADS // hp, nq),
        in_specs=[pl.BlockSpec((1, tq, hp * LANES), lambda b, h, i: (b, i, h)),
                  pl.BlockSpec((1, tq, hp * LANES), lambda b, h, i: (b, jnp.minimum(i + 1, nq - 1), h)),
                  pl.BlockSpec((1, S, hp * LANES), lambda b, h, i: (b, 0, h)),
                  pl.BlockSpec((1, S, hp * LANES), lambda b, h, i: (b, 0, h))],
        out_specs=pl.BlockSpec((1, tq, hp * V_DIM), lambda b, h, i: (b, i, h)),
        out_shape=jax.ShapeDtypeStruct((B, S, MLA_HEADS * V_DIM), BF16),
        scratch_shapes=[pltpu.VMEM((hp, LANES, S), BF16),
                        pltpu.VMEM((2, hp, LANES, tq), BF16),
                        pltpu.VMEM((2, hp, tk, tq), F32),
                        pltpu.VMEM((2, hp, 1, tq), F32),
                        pltpu.VMEM((hp, V_ROWS, tq), F32),
                        pltpu.VMEM((hp, 1, tq), F32)],
        compiler_params=_cparams(("arbitrary", "arbitrary", "arbitrary")),
        name="mla_attn",
    )(q, q, k, v)


def _lane_bcast(x, j):
    return jnp.broadcast_to(x[:, j:j + 1], x.shape)


CHUNKS_PER_STEP = 4


def _mlstm_kernel(qf_ref, ktf_ref, vf_ref, gcf_ref, grf_ref, qb_ref, ktb_ref, vb_ref, gcb_ref, grb_ref,
                  hf_ref, hb_ref, ct_ref, m_ref):
    @pl.when(pl.program_id(1) == 0)
    def _():
        ct_ref[...] = jnp.zeros_like(ct_ref)
        m_ref[...] = jnp.zeros_like(m_ref)

    L = MLSTM_CHUNK
    dh = MLSTM_HEAD_DIM
    nh = MLSTM_HEADS
    nsub = CHUNKS_PER_STEP
    rows = lax.broadcasted_iota(jnp.int32, (L, L), 0)
    cols = lax.broadcasted_iota(jnp.int32, (L, L), 1)
    lane1 = lax.broadcasted_iota(jnp.int32, (1, LANES), 1)
    dirs = ((qf_ref, ktf_ref, vf_ref, gcf_ref, grf_ref, hf_ref, False),
            (qb_ref, ktb_ref, vb_ref, gcb_ref, grb_ref, hb_ref, True))

    def chunk_rows(rev, sub):
        c = nsub - 1 - sub if rev else sub
        return slice(c * L, (c + 1) * L)

    m_row = m_ref[...]
    gate = []
    for sub in range(nsub):
        per_dir = []
        for d, (_, _, _, gc_ref, gr_ref, _, rev) in enumerate(dirs):
            rs = chunk_rows(rev, sub)
            gc = gc_ref[0, rs, :]
            cm = gc[:, :LANES]
            b = gc[:, LANES:]
            g = jnp.maximum(m_row, cm)
            floor = jnp.exp(-(b + g))
            end = 0 if rev else L - 1
            blast = b[end:end + 1, :]
            m_loc = blast + cm[end:end + 1, :]
            m_new = jnp.maximum(blast + m_row, m_loc)
            a_row = jnp.exp(blast + m_row - m_new)
            c_row = jnp.exp(m_loc - m_new)
            mask = (cols >= rows) if rev else (cols <= rows)
            per_dir.append((g, floor, a_row, c_row, m_new, mask, gr_ref[:, rs], m_row))
        gate.append(per_dir)
        m_row = jnp.where(lane1 < nh, per_dir[0][4], per_dir[1][4])
    m_ref[...] = m_row

    units = [(sub, d, h) for sub in range(nsub) for d in range(N_DIRS) for h in range(nh)]
    q_l, vx_l, sc_l, kte_l = [], [], [], []
    for sub, d, h in units:
        j = d * nh + h
        hs = slice(h * dh, (h + 1) * dh)
        rs = chunk_rows(dirs[d][6], sub)
        q = dirs[d][0][0, rs, hs]
        kt = dirs[d][1][hs, rs]
        gr = gate[sub][d][6]
        q_l.append(q)
        vx_l.append(jnp.concatenate([dirs[d][2][0, rs, hs], jnp.where(cols == j, 1.0, 0.0).astype(BF16)], axis=1))
        sc_l.append(_dot(q, kt))
        kte_l.append((kt.astype(F32) * gr[N_GATE_COLS + j:N_GATE_COLS + j + 1, :]).astype(BF16))
    s_l, a_l = [], []
    for u, (sub, d, h) in enumerate(units):
        j = d * nh + h
        g, _, _, _, _, mask, gr, m_in = gate[sub][d]
        g_rep = _lane_bcast(g, j)
        a_l.append(jnp.exp(m_in[:, j:j + 1] - g_rep))
        p = jnp.exp(jnp.where(mask, gr[j:j + 1, :] - g_rep, -jnp.inf))
        s_l.append((sc_l[u] * p).astype(BF16))
    nd_l = [_dot(s_l[u], vx_l[u]) for u in range(len(units))]
    cl_l = [_dot(kte_l[u], vx_l[u]) for u in range(len(units))]
    mem = [ct_ref[j] for j in range(N_GATE_COLS)]
    for sub in range(nsub):
        base = sub * N_GATE_COLS
        qc = [_dot(q_l[base + j], mem[j].astype(BF16)) for j in range(N_GATE_COLS)]
        for j in range(N_GATE_COLS):
            d, h = divmod(j, nh)
            u = base + j
            _, floor, a_row, c_row, _, _, _, _ = gate[sub][d]
            tot = nd_l[u] + jnp.concatenate([a_l[u], a_l[u]], axis=1) * qc[j]
            r = 1.0 / jnp.maximum(jnp.abs(tot[:, LANES:]), floor)
            rs = chunk_rows(dirs[d][6], sub)
            dirs[d][5][0, rs, h * dh:(h + 1) * dh] = tot[:, :LANES] * _lane_bcast(r, j)
            mem[j] = a_row[:, j:j + 1] * mem[j] + c_row[:, j:j + 1] * cl_l[u]
    for j in range(N_GATE_COLS):
        ct_ref[j] = mem[j]


def _mlstm(qm, kt, zv, gc, gr):
    B, S, w = qm.shape
    L2 = MLSTM_CHUNK * CHUNKS_PER_STEP
    ns = S // L2
    dh = MLSTM_HEAD_DIM
    fwd = lambda b, c: (b, c, 0)
    bwd = lambda b, c: (b, ns - 1 - c, 0)
    fwd_t = lambda b, c: (0, b * ns + c)
    bwd_t = lambda b, c: (0, b * ns + ns - 1 - c)
    ng2 = 2 * N_GATE_COLS

    def specs(im, im_t):
        return [pl.BlockSpec((1, L2, w), im), pl.BlockSpec((w, L2), im_t), pl.BlockSpec((1, L2, w), im),
                pl.BlockSpec((1, L2, 2 * LANES), im), pl.BlockSpec((ng2, L2), im_t)]

    return pl.pallas_call(
        _mlstm_kernel,
        grid=(B, ns),
        in_specs=specs(fwd, fwd_t) + specs(bwd, bwd_t),
        out_specs=[pl.BlockSpec((1, L2, w), fwd), pl.BlockSpec((1, L2, w), bwd)],
        out_shape=[jax.ShapeDtypeStruct((B, S, w), F32)] * 2,
        scratch_shapes=[pltpu.VMEM((N_GATE_COLS, dh, 2 * LANES), F32),
                        pltpu.VMEM((1, LANES), F32)],
        compiler_params=_cparams(("arbitrary", "arbitrary")),
        name="mlstm",
    )(qm, kt, zv, gc, gr, qm, kt, zv, gc, gr)


def _mix_xattn_kernel(x_ref, ya_ref, hf_ref, hb_ref, zo_ref, mn_ref, wom_ref, wol_ref, xn_ref, wxq_ref,
                      kx_ref, vx_ref, wxo_ref, o_ref, *, x_scale):
    dh = MLSTM_HEAD_DIM
    hs = hf_ref[...] + hb_ref[...]
    parts = []
    for h in range(MLSTM_HEADS):
        sl = slice(h * dh, (h + 1) * dh)
        parts.append(_rms(hs[:, sl], mn_ref[:, sl]))
    hn = jnp.concatenate(parts, axis=-1)
    yl = (jax.nn.sigmoid(zo_ref[...].astype(F32)) * hn).astype(BF16)
    x1 = x_ref[...] + _dot(ya_ref[...], wom_ref[...]) + _dot(yl, wol_ref[...])
    hq = _rms(x1, xn_ref[...]).astype(BF16)
    q = (_dot(hq, wxq_ref[...]) * x_scale).astype(BF16)
    xd = q.shape[-1] // X_HEADS
    sls = [slice(h * xd, (h + 1) * xd) for h in range(X_HEADS)]
    ss = [_dot_nt(q[:, sl], kx_ref[0, :, sl]) for sl in sls]
    ps = [jnp.exp2(s - jnp.max(s, axis=-1, keepdims=True)) for s in ss]
    ls = [jnp.sum(p, axis=-1, keepdims=True) for p in ps]
    pv = [_dot(p.astype(BF16), vx_ref[0, :, sl]) for p, sl in zip(ps, sls)]
    o = jnp.concatenate([(a * (1.0 / l)).astype(BF16) for a, l in zip(pv, ls)], axis=-1)
    o_ref[...] = x1 + _dot(o, wxo_ref[...])


def _mix_xattn(x2, ya, hf, hb, zo, mlstm_norm, wom, wol, xattn_norm, wxq, kx, vx, wxo, tm, S):
    T, D = x2.shape
    M = kx.shape[1]
    per_b = S // tm
    row = lambda n: pl.BlockSpec((tm, n), lambda i: (i, 0))
    mem_spec = pl.BlockSpec((1, M, D), lambda i: (i // per_b, 0, 0))
    x_scale = float((D // X_HEADS) ** -0.5 * np.log2(np.e))
    return pl.pallas_call(
        functools.partial(_mix_xattn_kernel, x_scale=x_scale),
        grid=(T // tm,),
        in_specs=[row(D), row(ya.shape[1]), row(hf.shape[1]), row(hb.shape[1]), row(zo.shape[1]),
                  _const_spec(mlstm_norm.shape), _const_spec(wom.shape), _const_spec(wol.shape),
                  _const_spec(xattn_norm.shape), _const_spec(wxq.shape), mem_spec, mem_spec,
                  _const_spec(wxo.shape)],
        out_specs=row(D),
        out_shape=jax.ShapeDtypeStruct((T, D), F32),
        compiler_params=_cparams(("arbitrary",)),
        name="mix_xattn",
    )(x2, ya, hf, hb, zo, mlstm_norm, wom, wol, xattn_norm, wxq, kx, vx, wxo)


def _ffn_kernel(x_ref, fn_ref, wg_ref, wu_ref, wd_ref, on_ref, o_ref, *, fc):
    x = x_ref[...]
    hn = _rms(x, fn_ref[...]).astype(BF16)
    dff = wg_ref.shape[1]
    n = dff // fc
    sls = [slice(j * fc, (j + 1) * fc) for j in range(n)]
    acc = x
    g, u = _dot(hn, wg_ref[:, sls[0]]), _dot(hn, wu_ref[:, sls[0]])
    for j in range(n):
        if j + 1 < n:
            g_next, u_next = _dot(hn, wg_ref[:, sls[j + 1]]), _dot(hn, wu_ref[:, sls[j + 1]])
        acc = acc + _dot((g * jax.nn.sigmoid(g) * u).astype(BF16), wd_ref[sls[j], :])
        if j + 1 < n:
            g, u = g_next, u_next
    o_ref[...] = _rms(acc, on_ref[...])


def _ffn(x2, ffn_norm, wg, wu, wd, final_norm, tm, fc):
    T, D = x2.shape
    row = pl.BlockSpec((tm, D), lambda i: (i, 0))
    return pl.pallas_call(
        functools.partial(_ffn_kernel, fc=fc),
        grid=(T // tm,),
        in_specs=[row, _const_spec(ffn_norm.shape), _const_spec(wg.shape), _const_spec(wu.shape),
                  _const_spec(wd.shape), _const_spec(final_norm.shape)],
        out_specs=row,
        out_shape=jax.ShapeDtypeStruct((T, D), F32),
        compiler_params=_cparams(("arbitrary",)),
        name="ffn",
    )(x2, ffn_norm, wg, wu, wd, final_norm)


def _head_blocks(wmat, per_head, take, put):
    rows = wmat.shape[0]
    w3 = wmat.reshape(rows, MLA_HEADS, per_head)[:, :, take[0]:take[1]]
    out = jnp.zeros((rows, MLA_HEADS, LANES), wmat.dtype)
    out = out.at[:, :, put:put + (take[1] - take[0])].set(w3)
    return out.reshape(rows, MLA_HEADS * LANES)


def _lane_block(wmat, put):
    rows, n = wmat.shape
    return jnp.zeros((rows, LANES), wmat.dtype).at[:, put:put + n].set(wmat)


def kernel(x, mem, positions, attn_norm, w_in, q_norm, w_uq, kv_norm, w_ukv, mlstm_conv, mlstm_gate_bias,
           mlstm_norm, w_out, xattn_norm, mem_norm, w_xq, w_xkv, w_xo, ffn_norm, w_gate_up, w_down,
           final_norm):
    B, S, D = x.shape
    T = B * S
    assert w_in.shape[0] == 1, "single-layer problem: the final norm is fused into the SwiGLU kernel"
    l = 0
    tm = 512
    x2 = x.reshape(T, D)

    cos, sin = _rope_tables(positions)
    vone = jnp.tile(jnp.zeros((1, LANES), F32).at[0, V_DIM].set(1.0), (1, MLA_HEADS))

    o_cq, o_ckv, o_kr = 0, Q_LORA, Q_LORA + KV_LORA
    o_qk = o_kr + ROPE_DIM
    o_v = o_qk + 2 * MLSTM_WIDTH
    o_o = o_v + MLSTM_WIDTH
    o_g = o_o + MLSTM_WIDTH
    nh = MLSTM_HEADS
    qd = NOPE_DIM + ROPE_DIM
    wi = w_in[l]
    kr1 = wi[:, o_kr:o_kr + HALF_ROPE]
    kr2 = wi[:, o_kr + HALF_ROPE:o_qk]
    kr_a = _lane_block(jnp.concatenate([kr1, kr2], axis=1), NOPE_DIM)
    kr_b = _lane_block(jnp.concatenate([kr2, kr1], axis=1), NOPE_DIM)
    wg = wi[:, o_g:]
    gate_perm = lambda m: jnp.concatenate(
        [m[:, 0:nh], m[:, 2 * nh:3 * nh], m[:, nh:2 * nh], m[:, 3 * nh:4 * nh]], axis=1)
    wa = jnp.concatenate([wi[:, o_cq:o_kr], kr_a, kr_b, _lane_block(gate_perm(wg), 0)], axis=1).astype(BF16)
    wb = wi[:, o_qk:o_g].astype(BF16)
    gbias = _lane_block(gate_perm(mlstm_gate_bias[l][None, :]), 0)
    wq = w_uq[l]
    wq_a = (_head_blocks(wq, qd, (0, NOPE_DIM), 0)
            + _head_blocks(wq, qd, (NOPE_DIM, qd), NOPE_DIM))
    wq_b = (_head_blocks(wq, qd, (NOPE_DIM + HALF_ROPE, qd), NOPE_DIM)
            + _head_blocks(wq, qd, (NOPE_DIM, NOPE_DIM + HALF_ROPE), NOPE_DIM + HALF_ROPE))
    wqq = jnp.concatenate([wq_a, wq_b], axis=1).astype(BF16)
    wkv = w_ukv[l]
    kvd = NOPE_DIM + V_DIM
    wkv2 = jnp.concatenate([_head_blocks(wkv, kvd, (0, NOPE_DIM), 0),
                            _head_blocks(wkv, kvd, (NOPE_DIM, kvd), 0)], axis=1).astype(BF16)
    wo = w_out[l]
    wom = wo[:MLA_HEADS * V_DIM].astype(BF16)
    wol = wo[MLA_HEADS * V_DIM:].astype(BF16)
    dff = w_down.shape[1]

    kx, vx = _mem_kv(mem, mem_norm[l][None, :], w_xkv[l].astype(BF16))
    q, k, v, qm, kt, zv, zo, gc, gr = _in_proj(
        x2, cos, sin, attn_norm[l][None, :], wa, wb, q_norm[l][None, :], wqq, kv_norm[l][None, :],
        wkv2, gbias, vone, mlstm_conv[l], tm, S)
    hw = MLA_HEADS * LANES
    ya = _mla_attention(q.reshape(B, S, hw), k.reshape(B, S, hw), v.reshape(B, S, hw), 2048, 512, 4)
    hf, hb = _mlstm(qm.reshape(B, S, -1), kt, zv.reshape(B, S, -1), gc.reshape(B, S, -1), gr)
    x2 = _mix_xattn(x2, ya.reshape(T, -1), hf.reshape(T, -1), hb.reshape(T, -1), zo,
                    mlstm_norm[l][None, :], wom, wol, xattn_norm[l][None, :], w_xq[l].astype(BF16),
                    kx, vx, w_xo[l].astype(BF16), tm, S)
    wgu = w_gate_up[l].astype(BF16)
    y = _ffn(x2, ffn_norm[l][None, :], wgu[:, :dff], wgu[:, dff:], w_down[l].astype(BF16),
             final_norm[None, :], tm, 256)
    return y.reshape(B, S, D)
```

```python
import functools

import numpy as np
import jax
import jax.numpy as jnp
from jax import lax
from jax.experimental import pallas as pl
from jax.experimental.pallas import tpu as pltpu

F32 = jnp.float32
BF16 = jnp.bfloat16

EPS = 1e-6
MLA_HEADS = 8
Q_LORA = 256
KV_LORA = 128
NOPE_DIM = 64
ROPE_DIM = 32
V_DIM = 64
ROPE_THETA = 10000.0
MLSTM_HEADS = 4
MLSTM_HEAD_DIM = 128
MLSTM_WIDTH = MLSTM_HEADS * MLSTM_HEAD_DIM
MLSTM_CHUNK = 128
X_HEADS = 4
LANES = 128
HALF_ROPE = ROPE_DIM // 2

VMEM_LIMIT = 56 * 1024 * 1024


def _cparams(sem):
    return pltpu.CompilerParams(dimension_semantics=sem, vmem_limit_bytes=VMEM_LIMIT)


def _rms(x, g):
    ms = jnp.mean(x * x, axis=-1, keepdims=True)
    return x * lax.rsqrt(ms + EPS) * g


def _dot(a, b):
    return jnp.dot(a, b, preferred_element_type=F32)


def _dot_nt(a, b):
    return lax.dot_general(a, b, (((1,), (1,)), ((), ())), preferred_element_type=F32)


def _dot_tn(a, b):
    return lax.dot_general(a, b, (((0,), (0,)), ((), ())), preferred_element_type=F32)


def _const_spec(shape):
    return pl.BlockSpec(shape, lambda *_: (0,) * len(shape))


def _mem_kv_kernel(mem_ref, g_ref, w_ref, k_ref, v_ref):
    d = mem_ref.shape[-1]
    mn = _rms(mem_ref[0], g_ref[...]).astype(BF16)
    kv = _dot(mn, w_ref[...])
    k_ref[0] = kv[:, :d].astype(BF16)
    v_ref[0] = kv[:, d:].astype(BF16)


def _mem_kv(mem, mem_norm, w_xkv):
    B, M, D = mem.shape
    return pl.pallas_call(
        _mem_kv_kernel,
        grid=(B,),
        in_specs=[pl.BlockSpec((1, M, D), lambda b: (b, 0, 0)),
                  _const_spec((1, D)), _const_spec((D, 2 * D))],
        out_specs=[pl.BlockSpec((1, M, D), lambda b: (b, 0, 0))] * 2,
        out_shape=[jax.ShapeDtypeStruct((B, M, D), BF16)] * 2,
        compiler_params=_cparams(("arbitrary",)),
        name="mem_kv",
    )(mem, mem_norm, w_xkv)


def _rope_kernel(ang_ref, cos_ref, sin_ref):
    ang = ang_ref[...]
    cos_ref[...] = jnp.cos(ang)
    sin_ref[...] = jnp.sin(ang)


def _rope_tables(positions):
    T = positions.size
    inv = ROPE_THETA ** (-jnp.arange(0, ROPE_DIM, 2, dtype=F32) / ROPE_DIM)
    ang = inv[:, None] * positions.astype(F32).reshape(1, T)
    bc = min(T, 8192)
    spec = pl.BlockSpec((HALF_ROPE, bc), lambda i: (0, i))
    return pl.pallas_call(
        _rope_kernel,
        grid=(T // bc,),
        in_specs=[spec],
        out_specs=[spec, spec],
        out_shape=[jax.ShapeDtypeStruct(ang.shape, F32)] * 2,
        compiler_params=_cparams(("arbitrary",)),
        name="rope_tab",
    )(ang)


V_ROWS = 80
HALO = 16
N_DIRS = 2
N_GATE_COLS = N_DIRS * MLSTM_HEADS


def _log_sigmoid(x):
    return jnp.minimum(x, 0.0) - jnp.log(1.0 + jnp.exp(-jnp.abs(x)))


def _chunk_scan(x, op, ident):
    n = x.shape[1]
    rows = lax.broadcasted_iota(jnp.int32, x.shape, 0)
    lanes = lax.broadcasted_iota(jnp.int32, x.shape, 1)
    fwd = rows < MLSTM_HEADS
    k = 1
    while k < n:
        down = pltpu.roll(x, k, axis=1)
        up = pltpu.roll(x, n - k, axis=1)
        shifted = jnp.where(fwd, jnp.where(lanes >= k, down, ident), jnp.where(lanes < n - k, up, ident))
        x = op(x, shifted)
        k *= 2
    return x


def _inproj_kernel(x_ref, xp_ref, xn_ref, cost_ref, sint_ref, an_ref, wa_ref, wb_ref, qn_ref,
                   wqt_ref, kvn_ref, wk_ref, wvt_ref, gb_ref, vone_ref, cw_ref,
                   qt_ref, k_ref, vt_ref, qm_ref, kt_ref, zv_ref, zo_ref, gc_ref, gr_ref,
                   *, q_scale, k_scale, tiles_per_seq):
    hw = MLA_HEADS * LANES
    tm = x_ref.shape[0]
    L = MLSTM_CHUNK
    w = MLSTM_WIDTH
    i = pl.program_id(0)
    first = (i % tiles_per_seq) == 0
    last = (i % tiles_per_seq) == tiles_per_seq - 1
    x_ext = jnp.concatenate([xp_ref[...], x_ref[...], xn_ref[...]], axis=0)
    h_ext = _rms(x_ext, an_ref[...]).astype(BF16)
    h = h_ext[HALO:HALO + tm]
    za = _dot(h, wa_ref[...])
    cq = za[:, :Q_LORA]
    ckv = za[:, Q_LORA:Q_LORA + KV_LORA]
    kra = za[:, Q_LORA + KV_LORA:Q_LORA + KV_LORA + LANES]
    krb = za[:, Q_LORA + KV_LORA + LANES:Q_LORA + KV_LORA + 2 * LANES]

    gt = (za[:, Q_LORA + KV_LORA + 2 * LANES:] + gb_ref[...]).T
    ng = N_GATE_COLS
    pad = jnp.zeros((L - ng, L), F32)
    for c in range(tm // L):
        cs = slice(c * L, (c + 1) * L)
        ic = gt[0:ng, cs]
        logf = _log_sigmoid(gt[ng:2 * ng, cs])
        bc = _chunk_scan(logf, jnp.add, 0.0)
        uc = ic - bc
        cmc = _chunk_scan(uc, jnp.maximum, -jnp.inf)
        ec = jnp.exp(uc - jnp.max(uc, axis=1, keepdims=True))
        gr_ref[0:ng, cs] = uc
        gr_ref[ng:2 * ng, cs] = ec
        gc_ref[cs, 0:LANES] = jnp.concatenate([cmc, pad], axis=0).T
        gc_ref[cs, LANES:2 * LANES] = jnp.concatenate([bc, pad], axis=0).T

    n_ext = tm + 2 * HALO
    cw = cw_ref[...]

    def conv_silu(cols):
        z = _dot(h_ext, wb_ref[:, cols])
        zc = z[HALO:HALO + tm]
        z_all = jnp.concatenate([jnp.where(first, 0.0, z[:HALO]), zc, jnp.where(last, 0.0, z[HALO + tm:])],
                                axis=0)
        zp = pltpu.roll(z_all, 1, axis=0)[HALO:HALO + tm]
        zn = pltpu.roll(z_all, n_ext - 1, axis=0)[HALO:HALO + tm]
        y = zp * cw[0:1, cols] + zc * cw[1:2, cols]
        y = y + zn * cw[2:3, cols]
        return y * jax.nn.sigmoid(y)

    qm_ref[...] = conv_silu(slice(0, w)).astype(BF16)
    kt_ref[...] = (conv_silu(slice(w, 2 * w)) * k_scale).T.astype(BF16)

    cqn = _rms(cq, qn_ref[...]).astype(BF16)
    ckvn = _rms(ckv, kvn_ref[...]).astype(BF16)
    cos_t = cost_ref[...]
    sin_t = sint_ref[...]
    tail_t = jnp.zeros((LANES - NOPE_DIM - ROPE_DIM, tm), F32)
    ct_t = jnp.concatenate([jnp.ones((NOPE_DIM, tm), F32), cos_t, cos_t, tail_t], axis=0)
    st_t = jnp.concatenate([jnp.zeros((NOPE_DIM, tm), F32), -sin_t, sin_t, tail_t], axis=0)
    qq_t = _dot_nt(wqt_ref[...], cqn)
    q_t = qq_t[:hw] * jnp.tile(ct_t, (MLA_HEADS, 1)) + qq_t[hw:] * jnp.tile(st_t, (MLA_HEADS, 1))
    qt_ref[...] = (q_t * q_scale).astype(BF16)
    krot = (kra.T * ct_t + krb.T * st_t).T
    k_ref[...] = (_dot(ckvn, wk_ref[...]) + jnp.tile(krot, (1, MLA_HEADS))).astype(BF16)
    vt_ref[...] = (_dot_nt(wvt_ref[...], ckvn) + vone_ref[...]).astype(BF16)

    zvo = _dot(h, wb_ref[:, 2 * w:])
    zv_ref[...] = zvo[:, :w].astype(BF16)
    zo_ref[...] = zvo[:, w:].astype(BF16)


def _in_proj(x2, cos_t, sin_t, attn_norm, wa, wb, q_norm, wqt, kv_norm, wk, wvt, gbias, vone, conv_w,
             tm, S):
    T, D = x2.shape
    hw = MLA_HEADS * LANES
    vr = MLA_HEADS * V_ROWS
    w = MLSTM_WIDTH
    per = tm // HALO
    nhb = T // HALO
    row = lambda n: pl.BlockSpec((tm, n), lambda i: (i, 0))
    col = lambda n: pl.BlockSpec((n, tm), lambda i: (0, i))
    prev = pl.BlockSpec((HALO, D), lambda i: (jnp.maximum(i * per - 1, 0), 0))
    nxt = pl.BlockSpec((HALO, D), lambda i: (jnp.minimum((i + 1) * per, nhb - 1), 0))
    out_specs = [col(hw), row(hw), col(vr), row(w), col(w), row(w), row(w), row(2 * LANES),
                 col(2 * N_GATE_COLS)]
    out_shape = [jax.ShapeDtypeStruct((hw, T), BF16), jax.ShapeDtypeStruct((T, hw), BF16),
                 jax.ShapeDtypeStruct((vr, T), BF16),
                 jax.ShapeDtypeStruct((T, w), BF16), jax.ShapeDtypeStruct((w, T), BF16),
                 jax.ShapeDtypeStruct((T, w), BF16), jax.ShapeDtypeStruct((T, w), BF16),
                 jax.ShapeDtypeStruct((T, 2 * LANES), F32), jax.ShapeDtypeStruct((2 * N_GATE_COLS, T), F32)]
    q_scale = float((NOPE_DIM + ROPE_DIM) ** -0.5 * np.log2(np.e))
    consts = (attn_norm, wa, wb, q_norm, wqt, kv_norm, wk, wvt, gbias, vone, conv_w)
    return pl.pallas_call(
        functools.partial(_inproj_kernel, q_scale=q_scale, k_scale=float(MLSTM_HEAD_DIM ** -0.5),
                          tiles_per_seq=S // tm),
        grid=(T // tm,),
        in_specs=[row(D), prev, nxt, col(HALF_ROPE), col(HALF_ROPE)]
        + [_const_spec(c.shape) for c in consts],
        out_specs=out_specs,
        out_shape=out_shape,
        compiler_params=_cparams(("arbitrary",)),
        name="in_proj",
    )(x2, x2, x2, cos_t, sin_t, *consts)


HEADS_PER_STEP = 2
Q_SUB = 256


def _mla_attn_kernel(qt_ref, qtn_ref, k_ref, vt_ref, o_ref, s_ref, mc_ref, acc_ref, m_ref,
                     *, tk, steps_per_iter):
    S = k_ref.shape[1]
    tq = qt_ref.shape[1]
    nk = S // tk
    assert steps_per_iter % 2 == 0 and nk % steps_per_iter == 0
    heads = range(HEADS_PER_STEP)
    qi = pl.program_id(2)
    nsub = tq // Q_SUB
    units = [(h, slice(i * Q_SUB, (i + 1) * Q_SUB)) for h in heads for i in range(nsub)]

    def scores(off, slot, h, qs, wrap=None):
        hr = slice(h * LANES, (h + 1) * LANES)
        qt = qt_ref[hr, qs]
        if wrap is not None:
            qt = jnp.where(wrap, qtn_ref[hr, qs], qt)
        st = _dot(k_ref[0, pl.ds(off, tk), hr], qt)
        s_ref[slot, h, :, qs] = st
        mc_ref[slot, h, :, qs] = jnp.max(st, axis=0, keepdims=True)

    def consume(off, slot, h, qs):
        m_old = m_ref[h, :, qs]
        m_new = jnp.maximum(m_old, mc_ref[slot, h, :, qs])
        alpha = jnp.exp2(m_old - m_new)
        p = jnp.exp2(s_ref[slot, h, :, qs] - m_new).astype(BF16)
        vt = vt_ref[h * V_ROWS:(h + 1) * V_ROWS, pl.ds(off, tk)]
        acc_ref[h, :, qs] = alpha * acc_ref[h, :, qs] + _dot(vt, p)
        m_ref[h, :, qs] = m_new

    @pl.when(qi == 0)
    def _():
        for h, qs in units:
            scores(0, 0, h, qs)

    for h in heads:
        m_ref[h] = jnp.full((1, tq), -jnp.inf, F32)
        acc_ref[h] = jnp.zeros((V_ROWS, tq), F32)

    def step(c, slot, may_wrap):
        if may_wrap:
            wrap = c + 1 == nk
            nxt = jnp.where(wrap, 0, c + 1)
        else:
            wrap, nxt = None, c + 1
        for h, qs in units:
            scores(pl.multiple_of(nxt * tk, tk), 1 - slot, h, qs, wrap)
            consume(pl.multiple_of(c * tk, tk), slot, h, qs)

    def body(jj, carry):
        for u in range(steps_per_iter):
            step(steps_per_iter * jj + u, u % 2, u == steps_per_iter - 1)
        return carry

    lax.fori_loop(0, nk // steps_per_iter, body, 0)
    outs = []
    for h in heads:
        acc = acc_ref[h]
        o_t = acc[0:V_DIM, :] * (1.0 / acc[V_DIM:V_DIM + 1, :])
        outs.append(o_t.T)
    o_ref[0] = jnp.concatenate(outs, axis=-1).astype(BF16)


def _mla_attention(qt, k, vt, B, tq, tk, steps_per_iter):
    S = k.shape[1]
    hp = HEADS_PER_STEP
    nq = S // tq
    return pl.pallas_call(
        functools.partial(_mla_attn_kernel, tk=tk, steps_per_iter=steps_per_iter),
        grid=(B, MLA_HEADS // hp, nq),
        in_specs=[pl.BlockSpec((hp * LANES, tq), lambda b, h, i: (h, b * nq + i)),
                  pl.BlockSpec((hp * LANES, tq), lambda b, h, i: (h, b * nq + jnp.minimum(i + 1, nq - 1))),
                  pl.BlockSpec((1, S, hp * LANES), lambda b, h, i: (b, 0, h)),
                  pl.BlockSpec((hp * V_ROWS, S), lambda b, h, i: (h, b))],
        out_specs=pl.BlockSpec((1, tq, hp * V_DIM), lambda b, h, i: (b, i, h)),
        out_shape=jax.ShapeDtypeStruct((B, S, MLA_HEADS * V_DIM), BF16),
        scratch_shapes=[pltpu.VMEM((2, hp, tk, tq), F32),
                        pltpu.VMEM((2, hp, 1, tq), F32),
                        pltpu.VMEM((hp, V_ROWS, tq), F32),
                        pltpu.VMEM((hp, 1, tq), F32)],
        compiler_params=_cparams(("arbitrary", "arbitrary", "arbitrary")),
        name="mla_attn",
    )(qt, qt, k, vt)


def _lane_bcast(x, j):
    return jnp.broadcast_to(x[:, j:j + 1], x.shape)


CHUNKS_PER_STEP = 4


def _mlstm_kernel(qf_ref, ktf_ref, vf_ref, gcf_ref, grf_ref, qb_ref, ktb_ref, vb_ref, gcb_ref, grb_ref,
                  hf_ref, hb_ref, ct_ref, m_ref):
    @pl.when(pl.program_id(1) == 0)
    def _():
        ct_ref[...] = jnp.zeros_like(ct_ref)
        m_ref[...] = jnp.zeros_like(m_ref)

    L = MLSTM_CHUNK
    dh = MLSTM_HEAD_DIM
    nh = MLSTM_HEADS
    nsub = CHUNKS_PER_STEP
    rows = lax.broadcasted_iota(jnp.int32, (L, L), 0)
    cols = lax.broadcasted_iota(jnp.int32, (L, L), 1)
    lane1 = lax.broadcasted_iota(jnp.int32, (1, LANES), 1)
    dirs = ((qf_ref, ktf_ref, vf_ref, gcf_ref, grf_ref, hf_ref, False),
            (qb_ref, ktb_ref, vb_ref, gcb_ref, grb_ref, hb_ref, True))

    def chunk_rows(rev, sub):
        c = nsub - 1 - sub if rev else sub
        return slice(c * L, (c + 1) * L)

    m_row = m_ref[...]
    gate = []
    for sub in range(nsub):
        per_dir = []
        for d, (_, _, _, gc_ref, gr_ref, _, rev) in enumerate(dirs):
            rs = chunk_rows(rev, sub)
            gc = gc_ref[0, rs, :]
            cm = gc[:, :LANES]
            b = gc[:, LANES:]
            g = jnp.maximum(m_row, cm)
            floor = jnp.exp(-(b + g))
            end = 0 if rev else L - 1
            blast = b[end:end + 1, :]
            m_loc = blast + cm[end:end + 1, :]
            m_new = jnp.maximum(blast + m_row, m_loc)
            a_row = jnp.exp(blast + m_row - m_new)
            c_row = jnp.exp(m_loc - m_new)
            mask = (cols >= rows) if rev else (cols <= rows)
            per_dir.append((g, floor, a_row, c_row, m_new, mask, gr_ref[:, rs], m_row))
        gate.append(per_dir)
        m_row = jnp.where(lane1 < nh, per_dir[0][4], per_dir[1][4])
    m_ref[...] = m_row

    units = [(sub, d, h) for sub in range(nsub) for d in range(N_DIRS) for h in range(nh)]
    q_l, vx_l, sc_l, kte_l = [], [], [], []
    for sub, d, h in units:
        j = d * nh + h
        hs = slice(h * dh, (h + 1) * dh)
        rs = chunk_rows(dirs[d][6], sub)
        q = dirs[d][0][0, rs, hs]
        kt = dirs[d][1][hs, rs]
        gr = gate[sub][d][6]
        q_l.append(q)
        vx_l.append(jnp.concatenate([dirs[d][2][0, rs, hs], jnp.where(cols == j, 1.0, 0.0).astype(BF16)], axis=1))
        sc_l.append(_dot(q, kt))
        kte_l.append((kt.astype(F32) * gr[N_GATE_COLS + j:N_GATE_COLS + j + 1, :]).astype(BF16))
    s_l, a_l = [], []
    for u, (sub, d, h) in enumerate(units):
        j = d * nh + h
        g, _, _, _, _, mask, gr, m_in = gate[sub][d]
        g_rep = _lane_bcast(g, j)
        a_l.append(jnp.exp(m_in[:, j:j + 1] - g_rep))
        p = jnp.exp(jnp.where(mask, gr[j:j + 1, :] - g_rep, -jnp.inf))
        s_l.append((sc_l[u] * p).astype(BF16))
    nd_l = [_dot(s_l[u], vx_l[u]) for u in range(len(units))]
    cl_l = [_dot(kte_l[u], vx_l[u]) for u in range(len(units))]
    mem = [ct_ref[j] for j in range(N_GATE_COLS)]
    for sub in range(nsub):
        base = sub * N_GATE_COLS
        qc = [_dot(q_l[base + j], mem[j].astype(BF16)) for j in range(N_GATE_COLS)]
        for j in range(N_GATE_COLS):
            d, h = divmod(j, nh)
            u = base + j
            _, floor, a_row, c_row, _, _, _, _ = gate[sub][d]
            tot = nd_l[u] + jnp.concatenate([a_l[u], a_l[u]], axis=1) * qc[j]
            r = 1.0 / jnp.maximum(jnp.abs(tot[:, LANES:]), floor)
            rs = chunk_rows(dirs[d][6], sub)
            dirs[d][5][0, rs, h * dh:(h + 1) * dh] = tot[:, :LANES] * _lane_bcast(r, j)
            mem[j] = a_row[:, j:j + 1] * mem[j] + c_row[:, j:j + 1] * cl_l[u]
    for j in range(N_GATE_COLS):
        ct_ref[j] = mem[j]


def _mlstm(qm, kt, zv, gc, gr):
    B, S, w = qm.shape
    L2 = MLSTM_CHUNK * CHUNKS_PER_STEP
    ns = S // L2
    dh = MLSTM_HEAD_DIM
    fwd = lambda b, c: (b, c, 0)
    bwd = lambda b, c: (b, ns - 1 - c, 0)
    fwd_t = lambda b, c: (0, b * ns + c)
    bwd_t = lambda b, c: (0, b * ns + ns - 1 - c)
    ng2 = 2 * N_GATE_COLS

    def specs(im, im_t):
        return [pl.BlockSpec((1, L2, w), im), pl.BlockSpec((w, L2), im_t), pl.BlockSpec((1, L2, w), im),
                pl.BlockSpec((1, L2, 2 * LANES), im), pl.BlockSpec((ng2, L2), im_t)]

    return pl.pallas_call(
        _mlstm_kernel,
        grid=(B, ns),
        in_specs=specs(fwd, fwd_t) + specs(bwd, bwd_t),
        out_specs=[pl.BlockSpec((1, L2, w), fwd), pl.BlockSpec((1, L2, w), bwd)],
        out_shape=[jax.ShapeDtypeStruct((B, S, w), F32)] * 2,
        scratch_shapes=[pltpu.VMEM((N_GATE_COLS, dh, 2 * LANES), F32),
                        pltpu.VMEM((1, LANES), F32)],
        compiler_params=_cparams(("arbitrary", "arbitrary")),
        name="mlstm",
    )(qm, kt, zv, gc, gr, qm, kt, zv, gc, gr)


def _mix_xattn_kernel(x_ref, ya_ref, hf_ref, hb_ref, zo_ref, mn_ref, wom_ref, wol_ref, xn_ref, wxq_ref,
                      kx_ref, vx_ref, wxo_ref, o_ref, *, x_scale):
    dh = MLSTM_HEAD_DIM
    hs = hf_ref[...] + hb_ref[...]
    parts = []
    for h in range(MLSTM_HEADS):
        sl = slice(h * dh, (h + 1) * dh)
        parts.append(_rms(hs[:, sl], mn_ref[:, sl]))
    hn = jnp.concatenate(parts, axis=-1)
    yl = (jax.nn.sigmoid(zo_ref[...].astype(F32)) * hn).astype(BF16)
    x1 = x_ref[...] + _dot(ya_ref[...], wom_ref[...]) + _dot(yl, wol_ref[...])
    hq = _rms(x1, xn_ref[...]).astype(BF16)
    q = (_dot(hq, wxq_ref[...]) * x_scale).astype(BF16)
    xd = q.shape[-1] // X_HEADS
    sls = [slice(h * xd, (h + 1) * xd) for h in range(X_HEADS)]
    ss = [_dot_nt(q[:, sl], kx_ref[0, :, sl]) for sl in sls]
    ps = [jnp.exp2(s - jnp.max(s, axis=-1, keepdims=True)) for s in ss]
    ls = [jnp.sum(p, axis=-1, keepdims=True) for p in ps]
    pv = [_dot(p.astype(BF16), vx_ref[0, :, sl]) for p, sl in zip(ps, sls)]
    o = jnp.concatenate([(a * (1.0 / l)).astype(BF16) for a, l in zip(pv, ls)], axis=-1)
    o_ref[...] = x1 + _dot(o, wxo_ref[...])


def _mix_xattn(x2, ya, hf, hb, zo, mlstm_norm, wom, wol, xattn_norm, wxq, kx, vx, wxo, tm, S):
    T, D = x2.shape
    M = kx.shape[1]
    per_b = S // tm
    row = lambda n: pl.BlockSpec((tm, n), lambda i: (i, 0))
    mem_spec = pl.BlockSpec((1, M, D), lambda i: (i // per_b, 0, 0))
    x_scale = float((D // X_HEADS) ** -0.5 * np.log2(np.e))
    return pl.pallas_call(
        functools.partial(_mix_xattn_kernel, x_scale=x_scale),
        grid=(T // tm,),
        in_specs=[row(D), row(ya.shape[1]), row(hf.shape[1]), row(hb.shape[1]), row(zo.shape[1]),
                  _const_spec(mlstm_norm.shape), _const_spec(wom.shape), _const_spec(wol.shape),
                  _const_spec(xattn_norm.shape), _const_spec(wxq.shape), mem_spec, mem_spec,
                  _const_spec(wxo.shape)],
        out_specs=row(D),
        out_shape=jax.ShapeDtypeStruct((T, D), F32),
        compiler_params=_cparams(("arbitrary",)),
        name="mix_xattn",
    )(x2, ya, hf, hb, zo, mlstm_norm, wom, wol, xattn_norm, wxq, kx, vx, wxo)


def _ffn_kernel(x_ref, fn_ref, wg_ref, wu_ref, wd_ref, on_ref, o_ref, *, fc):
    x = x_ref[...]
    hn = _rms(x, fn_ref[...]).astype(BF16)
    dff = wg_ref.shape[1]
    n = dff // fc
    sls = [slice(j * fc, (j + 1) * fc) for j in range(n)]
    acc = x
    g, u = _dot(hn, wg_ref[:, sls[0]]), _dot(hn, wu_ref[:, sls[0]])
    for j in range(n):
        if j + 1 < n:
            g_next, u_next = _dot(hn, wg_ref[:, sls[j + 1]]), _dot(hn, wu_ref[:, sls[j + 1]])
        acc = acc + _dot((g * jax.nn.sigmoid(g) * u).astype(BF16), wd_ref[sls[j], :])
        if j + 1 < n:
            g, u = g_next, u_next
    o_ref[...] = _rms(acc, on_ref[...])


def _ffn(x2, ffn_norm, wg, wu, wd, final_norm, tm, fc):
    T, D = x2.shape
    row = pl.BlockSpec((tm, D), lambda i: (i, 0))
    return pl.pallas_call(
        functools.partial(_ffn_kernel, fc=fc),
        grid=(T // tm,),
        in_specs=[row, _const_spec(ffn_norm.shape), _const_spec(wg.shape), _const_spec(wu.shape),
                  _const_spec(wd.shape), _const_spec(final_norm.shape)],
        out_specs=row,
        out_shape=jax.ShapeDtypeStruct((T, D), F32),
        compiler_params=_cparams(("arbitrary",)),
        name="ffn",
    )(x2, ffn_norm, wg, wu, wd, final_norm)


def _head_blocks(wmat, per_head, take, put):
    rows = wmat.shape[0]
    w3 = wmat.reshape(rows, MLA_HEADS, per_head)[:, :, take[0]:take[1]]
    out = jnp.zeros((rows, MLA_HEADS, LANES), wmat.dtype)
    out = out.at[:, :, put:put + (take[1] - take[0])].set(w3)
    return out.reshape(rows, MLA_HEADS * LANES)


def _lane_block(wmat, put):
    rows, n = wmat.shape
    return jnp.zeros((rows, LANES), wmat.dtype).at[:, put:put + n].set(wmat)


def kernel(x, mem, positions, attn_norm, w_in, q_norm, w_uq, kv_norm, w_ukv, mlstm_conv, mlstm_gate_bias,
           mlstm_norm, w_out, xattn_norm, mem_norm, w_xq, w_xkv, w_xo, ffn_norm, w_gate_up, w_down,
           final_norm):
    B, S, D = x.shape
    T = B * S
    assert w_in.shape[0] == 1, "single-layer problem: the final norm is fused into the SwiGLU kernel"
    l = 0
    tm = 512
    x2 = x.reshape(T, D)

    cos_t, sin_t = _rope_tables(positions)
    vone = jnp.zeros((MLA_HEADS, V_ROWS, 1), F32).at[:, V_DIM, 0].set(1.0).reshape(MLA_HEADS * V_ROWS, 1)
    o_cq, o_ckv, o_kr = 0, Q_LORA, Q_LORA + KV_LORA
    o_qk = o_kr + ROPE_DIM
    o_v = o_qk + 2 * MLSTM_WIDTH
    o_o = o_v + MLSTM_WIDTH
    o_g = o_o + MLSTM_WIDTH
    nh = MLSTM_HEADS
    qd = NOPE_DIM + ROPE_DIM
    wi = w_in[l]
    kr1 = wi[:, o_kr:o_kr + HALF_ROPE]
    kr2 = wi[:, o_kr + HALF_ROPE:o_qk]
    kr_a = _lane_block(jnp.concatenate([kr1, kr2], axis=1), NOPE_DIM)
    kr_b = _lane_block(jnp.concatenate([kr2, kr1], axis=1), NOPE_DIM)
    wg = wi[:, o_g:]
    gate_perm = lambda m: jnp.concatenate(
        [m[:, 0:nh], m[:, 2 * nh:3 * nh], m[:, nh:2 * nh], m[:, 3 * nh:4 * nh]], axis=1)
    wa = jnp.concatenate([wi[:, o_cq:o_kr], kr_a, kr_b, _lane_block(gate_perm(wg), 0)], axis=1).astype(BF16)
    wb = wi[:, o_qk:o_g].astype(BF16)
    gbias = _lane_block(gate_perm(mlstm_gate_bias[l][None, :]), 0)
    wq = w_uq[l]
    wq_a = (_head_blocks(wq, qd, (0, NOPE_DIM), 0)
            + _head_blocks(wq, qd, (NOPE_DIM, qd), NOPE_DIM))
    wq_b = (_head_blocks(wq, qd, (NOPE_DIM + HALF_ROPE, qd), NOPE_DIM)
            + _head_blocks(wq, qd, (NOPE_DIM, NOPE_DIM + HALF_ROPE), NOPE_DIM + HALF_ROPE))
    wqt = jnp.concatenate([wq_a, wq_b], axis=1).T.astype(BF16)
    wkv = w_ukv[l]
    kvd = NOPE_DIM + V_DIM
    wk = _head_blocks(wkv, kvd, (0, NOPE_DIM), 0).astype(BF16)
    wv3 = wkv.reshape(KV_LORA, MLA_HEADS, kvd)[:, :, NOPE_DIM:]
    wvt = jnp.zeros((MLA_HEADS, V_ROWS, KV_LORA), F32).at[:, :V_DIM, :].set(
        wv3.transpose(1, 2, 0)).reshape(MLA_HEADS * V_ROWS, KV_LORA).astype(BF16)
    wo = w_out[l]
    wom = wo[:MLA_HEADS * V_DIM].astype(BF16)
    wol = wo[MLA_HEADS * V_DIM:].astype(BF16)
    dff = w_down.shape[1]

    kx, vx = _mem_kv(mem, mem_norm[l][None, :], w_xkv[l].astype(BF16))
    qt, k, vt, qm, kt, zv, zo, gc, gr = _in_proj(
        x2, cos_t, sin_t, attn_norm[l][None, :], wa, wb, q_norm[l][None, :], wqt, kv_norm[l][None, :],
        wk, wvt, gbias, vone, mlstm_conv[l], tm, S)
    ya = _mla_attention(qt, k.reshape(B, S, MLA_HEADS * LANES), vt, B, 2048, 512, 4)
    hf, hb = _mlstm(qm.reshape(B, S, -1), kt, zv.reshape(B, S, -1), gc.reshape(B, S, -1), gr)
    x2 = _mix_xattn(x2, ya.reshape(T, -1), hf.reshape(T, -1), hb.reshape(T, -1), zo,
                    mlstm_norm[l][None, :], wom, wol, xattn_norm[l][None, :], w_xq[l].astype(BF16),
                    kx, vx, w_xo[l].astype(BF16), tm, S)
    wgu = w_gate_up[l].astype(BF16)
    y = _ffn(x2, ffn_norm[l][None, :], wgu[:, :dff], wgu[:, dff:], w_down[l].astype(BF16),
             final_norm[None, :], tm, 256)
    return y.reshape(B, S, D)
```

```python
import functools

import numpy as np
import jax
import jax.numpy as jnp
from jax import lax
from jax.experimental import pallas as pl
from jax.experimental.pallas import tpu as pltpu

F32 = jnp.float32
BF16 = jnp.bfloat16

EPS = 1e-6
MLA_HEADS = 8
Q_LORA = 256
KV_LORA = 128
NOPE_DIM = 64
ROPE_DIM = 32
V_DIM = 64
ROPE_THETA = 10000.0
MLSTM_HEADS = 4
MLSTM_HEAD_DIM = 128
MLSTM_WIDTH = MLSTM_HEADS * MLSTM_HEAD_DIM
MLSTM_CHUNK = 128
X_HEADS = 4
LANES = 128
HALF_ROPE = ROPE_DIM // 2

VMEM_LIMIT = 56 * 1024 * 1024

TOKEN_TILE = 512
FFN_CHUNK = 256
ATTN_Q_TILE = 1024
ATTN_K_CHUNK = 512
ATTN_STEPS_PER_ITER = 4


def _cparams(sem):
    return pltpu.CompilerParams(dimension_semantics=sem, vmem_limit_bytes=VMEM_LIMIT)


def _rms(x, g):
    ms = jnp.mean(x * x, axis=-1, keepdims=True)
    return x * lax.rsqrt(ms + EPS) * g


def _dot(a, b):
    return jnp.dot(a, b, preferred_element_type=F32)


def _dot_nt(a, b):
    return lax.dot_general(a, b, (((1,), (1,)), ((), ())), preferred_element_type=F32)


def _const_spec(shape):
    return pl.BlockSpec(shape, lambda *_: (0,) * len(shape))


def _mem_kv_kernel(mem_ref, g_ref, w_ref, k_ref, v_ref):
    d = mem_ref.shape[-1]
    mn = _rms(mem_ref[0], g_ref[...]).astype(BF16)
    kv = _dot(mn, w_ref[...])
    k_ref[0] = kv[:, :d].astype(BF16)
    v_ref[0] = kv[:, d:].astype(BF16)


def _mem_kv(mem, mem_norm, w_xkv):
    B, M, D = mem.shape
    return pl.pallas_call(
        _mem_kv_kernel,
        grid=(B,),
        in_specs=[pl.BlockSpec((1, M, D), lambda b: (b, 0, 0)),
                  _const_spec((1, D)), _const_spec((D, 2 * D))],
        out_specs=[pl.BlockSpec((1, M, D), lambda b: (b, 0, 0))] * 2,
        out_shape=[jax.ShapeDtypeStruct((B, M, D), BF16)] * 2,
        compiler_params=_cparams(("arbitrary",)),
        name="mem_kv",
    )(mem, mem_norm, w_xkv)


def _rope_kernel(ang_ref, cos_ref, sin_ref):
    ang = ang_ref[...]
    cos_ref[...] = jnp.cos(ang)
    sin_ref[...] = jnp.sin(ang)


def _rope_tables(positions):
    T = positions.size
    inv = ROPE_THETA ** (-jnp.arange(0, ROPE_DIM, 2, dtype=F32) / ROPE_DIM)
    ang = inv[:, None] * positions.astype(F32).reshape(1, T)
    bc = min(T, 8192)
    spec = pl.BlockSpec((HALF_ROPE, bc), lambda i: (0, i))
    return pl.pallas_call(
        _rope_kernel,
        grid=(T // bc,),
        in_specs=[spec],
        out_specs=[spec, spec],
        out_shape=[jax.ShapeDtypeStruct(ang.shape, F32)] * 2,
        compiler_params=_cparams(("arbitrary",)),
        name="rope_tab",
    )(ang)


V_ROWS = 80
HALO = 16
N_DIRS = 2
N_GATE_COLS = N_DIRS * MLSTM_HEADS


def _log_sigmoid(x):
    return jnp.minimum(x, 0.0) - jnp.log(1.0 + jnp.exp(-jnp.abs(x)))


def _chunk_scan(x, op, ident):
    n = x.shape[1]
    rows = lax.broadcasted_iota(jnp.int32, x.shape, 0)
    lanes = lax.broadcasted_iota(jnp.int32, x.shape, 1)
    fwd = rows < MLSTM_HEADS
    k = 1
    while k < n:
        down = pltpu.roll(x, k, axis=1)
        up = pltpu.roll(x, n - k, axis=1)
        shifted = jnp.where(fwd, jnp.where(lanes >= k, down, ident), jnp.where(lanes < n - k, up, ident))
        x = op(x, shifted)
        k *= 2
    return x


def _inproj_kernel(x_ref, xp_ref, xn_ref, cost_ref, sint_ref, an_ref, wa_ref, wb_ref, qn_ref,
                   wqt_ref, kvn_ref, wk_ref, wvt_ref, gb_ref, vone_ref, cw_ref,
                   qt_ref, k_ref, vt_ref, qm_ref, kt_ref, zv_ref, zo_ref, gc_ref, gr_ref,
                   *, q_scale, k_scale, tiles_per_seq):
    hw = MLA_HEADS * LANES
    tm = x_ref.shape[0]
    L = MLSTM_CHUNK
    w = MLSTM_WIDTH
    i = pl.program_id(0)
    first = (i % tiles_per_seq) == 0
    last = (i % tiles_per_seq) == tiles_per_seq - 1
    x_ext = jnp.concatenate([xp_ref[...], x_ref[...], xn_ref[...]], axis=0)
    h_ext = _rms(x_ext, an_ref[...]).astype(BF16)
    h = h_ext[HALO:HALO + tm]
    za = _dot(h, wa_ref[...])
    cq = za[:, :Q_LORA]
    ckv = za[:, Q_LORA:Q_LORA + KV_LORA]
    kra = za[:, Q_LORA + KV_LORA:Q_LORA + KV_LORA + LANES]
    krb = za[:, Q_LORA + KV_LORA + LANES:Q_LORA + KV_LORA + 2 * LANES]

    gt = (za[:, Q_LORA + KV_LORA + 2 * LANES:] + gb_ref[...]).T
    ng = N_GATE_COLS
    pad = jnp.zeros((L - ng, L), F32)
    for c in range(tm // L):
        cs = slice(c * L, (c + 1) * L)
        ic = gt[0:ng, cs]
        logf = _log_sigmoid(gt[ng:2 * ng, cs])
        bc = _chunk_scan(logf, jnp.add, 0.0)
        uc = ic - bc
        cmc = _chunk_scan(uc, jnp.maximum, -jnp.inf)
        ec = jnp.exp(uc - jnp.max(uc, axis=1, keepdims=True))
        gr_ref[0:ng, cs] = uc
        gr_ref[ng:2 * ng, cs] = ec
        gc_ref[cs, 0:LANES] = jnp.concatenate([cmc, pad], axis=0).T
        gc_ref[cs, LANES:2 * LANES] = jnp.concatenate([bc, pad], axis=0).T

    n_ext = tm + 2 * HALO
    cw = cw_ref[...]

    def conv_silu(cols):
        z = _dot(h_ext, wb_ref[:, cols])
        zc = z[HALO:HALO + tm]
        z_all = jnp.concatenate([jnp.where(first, 0.0, z[:HALO]), zc, jnp.where(last, 0.0, z[HALO + tm:])],
                                axis=0)
        zp = pltpu.roll(z_all, 1, axis=0)[HALO:HALO + tm]
        zn = pltpu.roll(z_all, n_ext - 1, axis=0)[HALO:HALO + tm]
        y = zp * cw[0:1, cols] + zc * cw[1:2, cols]
        y = y + zn * cw[2:3, cols]
        return y * jax.nn.sigmoid(y)

    qm_ref[...] = conv_silu(slice(0, w)).astype(BF16)
    kt_ref[...] = (conv_silu(slice(w, 2 * w)) * k_scale).T.astype(BF16)

    cqn = _rms(cq, qn_ref[...]).astype(BF16)
    ckvn = _rms(ckv, kvn_ref[...]).astype(BF16)
    cos_t = cost_ref[...]
    sin_t = sint_ref[...]
    tail_t = jnp.zeros((LANES - NOPE_DIM - ROPE_DIM, tm), F32)
    ct_t = jnp.concatenate([jnp.ones((NOPE_DIM, tm), F32), cos_t, cos_t, tail_t], axis=0)
    st_t = jnp.concatenate([jnp.zeros((NOPE_DIM, tm), F32), -sin_t, sin_t, tail_t], axis=0)
    qq_t = _dot_nt(wqt_ref[...], cqn)
    q_t = qq_t[:hw] * jnp.tile(ct_t, (MLA_HEADS, 1)) + qq_t[hw:] * jnp.tile(st_t, (MLA_HEADS, 1))
    qt_ref[...] = (q_t * q_scale).astype(BF16)
    krot = (kra.T * ct_t + krb.T * st_t).T
    k_ref[...] = (_dot(ckvn, wk_ref[...]) + jnp.tile(krot, (1, MLA_HEADS))).astype(BF16)
    vt_ref[...] = (_dot_nt(wvt_ref[...], ckvn) + vone_ref[...]).astype(BF16)

    zvo = _dot(h, wb_ref[:, 2 * w:])
    zv_ref[...] = zvo[:, :w].astype(BF16)
    zo_ref[...] = zvo[:, w:].astype(BF16)


def _in_proj(x2, cos_t, sin_t, attn_norm, wa, wb, q_norm, wqt, kv_norm, wk, wvt, gbias, vone, conv_w,
             tm, S):
    T, D = x2.shape
    hw = MLA_HEADS * LANES
    vr = MLA_HEADS * V_ROWS
    w = MLSTM_WIDTH
    per = tm // HALO
    nhb = T // HALO
    row = lambda n: pl.BlockSpec((tm, n), lambda i: (i, 0))
    col = lambda n: pl.BlockSpec((n, tm), lambda i: (0, i))
    prev = pl.BlockSpec((HALO, D), lambda i: (jnp.maximum(i * per - 1, 0), 0))
    nxt = pl.BlockSpec((HALO, D), lambda i: (jnp.minimum((i + 1) * per, nhb - 1), 0))
    out_specs = [col(hw), row(hw), col(vr), row(w), col(w), row(w), row(w), row(2 * LANES),
                 col(2 * N_GATE_COLS)]
    out_shape = [jax.ShapeDtypeStruct((hw, T), BF16), jax.ShapeDtypeStruct((T, hw), BF16),
                 jax.ShapeDtypeStruct((vr, T), BF16),
                 jax.ShapeDtypeStruct((T, w), BF16), jax.ShapeDtypeStruct((w, T), BF16),
                 jax.ShapeDtypeStruct((T, w), BF16), jax.ShapeDtypeStruct((T, w), BF16),
                 jax.ShapeDtypeStruct((T, 2 * LANES), F32), jax.ShapeDtypeStruct((2 * N_GATE_COLS, T), F32)]
    q_scale = float((NOPE_DIM + ROPE_DIM) ** -0.5 * np.log2(np.e))
    consts = (attn_norm, wa, wb, q_norm, wqt, kv_norm, wk, wvt, gbias, vone, conv_w)
    return pl.pallas_call(
        functools.partial(_inproj_kernel, q_scale=q_scale, k_scale=float(MLSTM_HEAD_DIM ** -0.5),
                          tiles_per_seq=S // tm),
        grid=(T // tm,),
        in_specs=[row(D), prev, nxt, col(HALF_ROPE), col(HALF_ROPE)]
        + [_const_spec(c.shape) for c in consts],
        out_specs=out_specs,
        out_shape=out_shape,
        compiler_params=_cparams(("arbitrary",)),
        name="in_proj",
    )(x2, x2, x2, cos_t, sin_t, *consts)


HEADS_PER_STEP = 4
Q_SUB = 256


def _mla_attn_kernel(qt_ref, qtn_ref, k_ref, vt_ref, o_ref, s_ref, mc_ref, acc_ref, m_ref,
                     *, tk, steps_per_iter):
    S = k_ref.shape[1]
    tq = qt_ref.shape[1]
    nk = S // tk
    assert steps_per_iter % 2 == 0 and nk % steps_per_iter == 0
    heads = range(HEADS_PER_STEP)
    qi = pl.program_id(2)
    nsub = tq // Q_SUB
    units = [(h, slice(i * Q_SUB, (i + 1) * Q_SUB)) for h in heads for i in range(nsub)]

    def scores(off, slot, h, qs, wrap=None):
        hr = slice(h * LANES, (h + 1) * LANES)
        qt = qt_ref[hr, qs]
        if wrap is not None:
            qt = jnp.where(wrap, qtn_ref[hr, qs], qt)
        st = _dot(k_ref[0, pl.ds(off, tk), hr], qt)
        s_ref[slot, h, :, qs] = st
        mc_ref[slot, h, :, qs] = jnp.max(st, axis=0, keepdims=True)

    def consume(off, slot, h, qs):
        m_old = m_ref[h, :, qs]
        m_new = jnp.maximum(m_old, mc_ref[slot, h, :, qs])
        alpha = jnp.exp2(m_old - m_new)
        p = jnp.exp2(s_ref[slot, h, :, qs] - m_new).astype(BF16)
        vt = vt_ref[h * V_ROWS:(h + 1) * V_ROWS, pl.ds(off, tk)]
        acc_ref[h, :, qs] = alpha * acc_ref[h, :, qs] + _dot(vt, p)
        m_ref[h, :, qs] = m_new

    @pl.when(qi == 0)
    def _():
        for h, qs in units:
            scores(0, 0, h, qs)

    for h in heads:
        m_ref[h] = jnp.full((1, tq), -jnp.inf, F32)
        acc_ref[h] = jnp.zeros((V_ROWS, tq), F32)

    def step(c, slot, may_wrap):
        if may_wrap:
            wrap = c + 1 == nk
            nxt = jnp.where(wrap, 0, c + 1)
        else:
            wrap, nxt = None, c + 1
        for h, qs in units:
            scores(pl.multiple_of(nxt * tk, tk), 1 - slot, h, qs, wrap)
            consume(pl.multiple_of(c * tk, tk), slot, h, qs)

    def body(jj, carry):
        for u in range(steps_per_iter):
            step(steps_per_iter * jj + u, u % 2, u == steps_per_iter - 1)
        return carry

    lax.fori_loop(0, nk // steps_per_iter, body, 0)
    outs = []
    for h in heads:
        acc = acc_ref[h]
        o_t = acc[0:V_DIM, :] * (1.0 / acc[V_DIM:V_DIM + 1, :])
        outs.append(o_t.T)
    o_ref[0] = jnp.concatenate(outs, axis=-1).astype(BF16)


def _mla_attention(qt, k, vt, B, tq, tk, steps_per_iter):
    S = k.shape[1]
    hp = HEADS_PER_STEP
    nq = S // tq
    return pl.pallas_call(
        functools.partial(_mla_attn_kernel, tk=tk, steps_per_iter=steps_per_iter),
        grid=(B, MLA_HEADS // hp, nq),
        in_specs=[pl.BlockSpec((hp * LANES, tq), lambda b, h, i: (h, b * nq + i)),
                  pl.BlockSpec((hp * LANES, tq), lambda b, h, i: (h, b * nq + jnp.minimum(i + 1, nq - 1))),
                  pl.BlockSpec((1, S, hp * LANES), lambda b, h, i: (b, 0, h)),
                  pl.BlockSpec((hp * V_ROWS, S), lambda b, h, i: (h, b))],
        out_specs=pl.BlockSpec((1, tq, hp * V_DIM), lambda b, h, i: (b, i, h)),
        out_shape=jax.ShapeDtypeStruct((B, S, MLA_HEADS * V_DIM), BF16),
        scratch_shapes=[pltpu.VMEM((2, hp, tk, tq), F32),
                        pltpu.VMEM((2, hp, 1, tq), F32),
                        pltpu.VMEM((hp, V_ROWS, tq), F32),
                        pltpu.VMEM((hp, 1, tq), F32)],
        compiler_params=_cparams(("arbitrary", "arbitrary", "arbitrary")),
        name="mla_attn",
    )(qt, qt, k, vt)


def _lane_bcast(x, j):
    return jnp.broadcast_to(x[:, j:j + 1], x.shape)


CHUNKS_PER_STEP = 8


def _mlstm_kernel(qf_ref, ktf_ref, vf_ref, gcf_ref, grf_ref, qb_ref, ktb_ref, vb_ref, gcb_ref, grb_ref,
                  hf_ref, hb_ref, ct_ref, m_ref):
    @pl.when(pl.program_id(1) == 0)
    def _():
        ct_ref[...] = jnp.zeros_like(ct_ref)
        m_ref[...] = jnp.zeros_like(m_ref)

    L = MLSTM_CHUNK
    dh = MLSTM_HEAD_DIM
    nh = MLSTM_HEADS
    nsub = CHUNKS_PER_STEP
    rows = lax.broadcasted_iota(jnp.int32, (L, L), 0)
    cols = lax.broadcasted_iota(jnp.int32, (L, L), 1)
    lane1 = lax.broadcasted_iota(jnp.int32, (1, LANES), 1)
    dirs = ((qf_ref, ktf_ref, vf_ref, gcf_ref, grf_ref, hf_ref, False),
            (qb_ref, ktb_ref, vb_ref, gcb_ref, grb_ref, hb_ref, True))

    def chunk_rows(rev, sub):
        c = nsub - 1 - sub if rev else sub
        return slice(c * L, (c + 1) * L)

    m_row = m_ref[...]
    gate = []
    for sub in range(nsub):
        per_dir = []
        for d, (_, _, _, gc_ref, gr_ref, _, rev) in enumerate(dirs):
            rs = chunk_rows(rev, sub)
            gc = gc_ref[0, rs, :]
            cm = gc[:, :LANES]
            b = gc[:, LANES:]
            g = jnp.maximum(m_row, cm)
            floor = jnp.exp(-(b + g))
            end = 0 if rev else L - 1
            blast = b[end:end + 1, :]
            m_loc = blast + cm[end:end + 1, :]
            m_new = jnp.maximum(blast + m_row, m_loc)
            a_row = jnp.exp(blast + m_row - m_new)
            c_row = jnp.exp(m_loc - m_new)
            mask = (cols >= rows) if rev else (cols <= rows)
            per_dir.append((g, floor, a_row, c_row, m_new, mask, gr_ref[:, rs], m_row))
        gate.append(per_dir)
        m_row = jnp.where(lane1 < nh, per_dir[0][4], per_dir[1][4])
    m_ref[...] = m_row

    units = [(sub, d, h) for sub in range(nsub) for d in range(N_DIRS) for h in range(nh)]
    q_l, vx_l, sc_l, kte_l = [], [], [], []
    for sub, d, h in units:
        j = d * nh + h
        hs = slice(h * dh, (h + 1) * dh)
        rs = chunk_rows(dirs[d][6], sub)
        q = dirs[d][0][0, rs, hs]
        kt = dirs[d][1][hs, rs]
        gr = gate[sub][d][6]
        q_l.append(q)
        vx_l.append(jnp.concatenate([dirs[d][2][0, rs, hs], jnp.where(cols == j, 1.0, 0.0).astype(BF16)], axis=1))
        sc_l.append(_dot(q, kt))
        kte_l.append((kt.astype(F32) * gr[N_GATE_COLS + j:N_GATE_COLS + j + 1, :]).astype(BF16))
    s_l, a_l = [], []
    for u, (sub, d, h) in enumerate(units):
        j = d * nh + h
        g, _, _, _, _, mask, gr, m_in = gate[sub][d]
        g_rep = _lane_bcast(g, j)
        a_l.append(jnp.exp(m_in[:, j:j + 1] - g_rep))
        p = jnp.exp(jnp.where(mask, gr[j:j + 1, :] - g_rep, -jnp.inf))
        s_l.append((sc_l[u] * p).astype(BF16))
    nd_l = [_dot(s_l[u], vx_l[u]) for u in range(len(units))]
    cl_l = [_dot(kte_l[u], vx_l[u]) for u in range(len(units))]
    mem = [ct_ref[j] for j in range(N_GATE_COLS)]
    for sub in range(nsub):
        base = sub * N_GATE_COLS
        qc = [_dot(q_l[base + j], mem[j].astype(BF16)) for j in range(N_GATE_COLS)]
        for j in range(N_GATE_COLS):
            d, h = divmod(j, nh)
            u = base + j
            _, floor, a_row, c_row, _, _, _, _ = gate[sub][d]
            tot = nd_l[u] + jnp.concatenate([a_l[u], a_l[u]], axis=1) * qc[j]
            r = 1.0 / jnp.maximum(jnp.abs(tot[:, LANES:]), floor)
            rs = chunk_rows(dirs[d][6], sub)
            dirs[d][5][0, rs, h * dh:(h + 1) * dh] = tot[:, :LANES] * _lane_bcast(r, j)
            mem[j] = a_row[:, j:j + 1] * mem[j] + c_row[:, j:j + 1] * cl_l[u]
    for j in range(N_GATE_COLS):
        ct_ref[j] = mem[j]


def _mlstm(qm, kt, zv, gc, gr):
    B, S, w = qm.shape
    L2 = MLSTM_CHUNK * CHUNKS_PER_STEP
    ns = S // L2
    dh = MLSTM_HEAD_DIM
    fwd = lambda b, c: (b, c, 0)
    bwd = lambda b, c: (b, ns - 1 - c, 0)
    fwd_t = lambda b, c: (0, b * ns + c)
    bwd_t = lambda b, c: (0, b * ns + ns - 1 - c)
    ng2 = 2 * N_GATE_COLS

    def specs(im, im_t):
        return [pl.BlockSpec((1, L2, w), im), pl.BlockSpec((w, L2), im_t), pl.BlockSpec((1, L2, w), im),
                pl.BlockSpec((1, L2, 2 * LANES), im), pl.BlockSpec((ng2, L2), im_t)]

    return pl.pallas_call(
        _mlstm_kernel,
        grid=(B, ns),
        in_specs=specs(fwd, fwd_t) + specs(bwd, bwd_t),
        out_specs=[pl.BlockSpec((1, L2, w), fwd), pl.BlockSpec((1, L2, w), bwd)],
        out_shape=[jax.ShapeDtypeStruct((B, S, w), F32)] * 2,
        scratch_shapes=[pltpu.VMEM((N_GATE_COLS, dh, 2 * LANES), F32),
                        pltpu.VMEM((1, LANES), F32)],
        compiler_params=_cparams(("arbitrary", "arbitrary")),
        name="mlstm",
    )(qm, kt, zv, gc, gr, qm, kt, zv, gc, gr)


def _mix_xattn_kernel(x_ref, ya_ref, hf_ref, hb_ref, zo_ref, mn_ref, wom_ref, wol_ref, xn_ref, wxq_ref,
                      kx_ref, vx_ref, wxo_ref, o_ref, *, x_scale):
    dh = MLSTM_HEAD_DIM
    hs = hf_ref[...] + hb_ref[...]
    parts = []
    for h in range(MLSTM_HEADS):
        sl = slice(h * dh, (h + 1) * dh)
        parts.append(_rms(hs[:, sl], mn_ref[:, sl]))
    hn = jnp.concatenate(parts, axis=-1)
    yl = (jax.nn.sigmoid(zo_ref[...].astype(F32)) * hn).astype(BF16)
    x1 = x_ref[...] + _dot(ya_ref[...], wom_ref[...]) + _dot(yl, wol_ref[...])
    hq = _rms(x1, xn_ref[...]).astype(BF16)
    q = (_dot(hq, wxq_ref[...]) * x_scale).astype(BF16)
    xd = q.shape[-1] // X_HEADS
    sls = [slice(h * xd, (h + 1) * xd) for h in range(X_HEADS)]
    ss = [_dot_nt(q[:, sl], kx_ref[0, :, sl]) for sl in sls]
    ps = [jnp.exp2(s - jnp.max(s, axis=-1, keepdims=True)) for s in ss]
    ls = [jnp.sum(p, axis=-1, keepdims=True) for p in ps]
    pv = [_dot(p.astype(BF16), vx_ref[0, :, sl]) for p, sl in zip(ps, sls)]
    o = jnp.concatenate([(a * (1.0 / l)).astype(BF16) for a, l in zip(pv, ls)], axis=-1)
    o_ref[...] = x1 + _dot(o, wxo_ref[...])


def _mix_xattn(x2, ya, hf, hb, zo, mlstm_norm, wom, wol, xattn_norm, wxq, kx, vx, wxo, tm, S):
    T, D = x2.shape
    M = kx.shape[1]
    per_b = S // tm
    row = lambda n: pl.BlockSpec((tm, n), lambda i: (i, 0))
    mem_spec = pl.BlockSpec((1, M, D), lambda i: (i // per_b, 0, 0))
    x_scale = float((D // X_HEADS) ** -0.5 * np.log2(np.e))
    return pl.pallas_call(
        functools.partial(_mix_xattn_kernel, x_scale=x_scale),
        grid=(T // tm,),
        in_specs=[row(D), row(ya.shape[1]), row(hf.shape[1]), row(hb.shape[1]), row(zo.shape[1]),
                  _const_spec(mlstm_norm.shape), _const_spec(wom.shape), _const_spec(wol.shape),
                  _const_spec(xattn_norm.shape), _const_spec(wxq.shape), mem_spec, mem_spec,
                  _const_spec(wxo.shape)],
        out_specs=row(D),
        out_shape=jax.ShapeDtypeStruct((T, D), F32),
        compiler_params=_cparams(("arbitrary",)),
        name="mix_xattn",
    )(x2, ya, hf, hb, zo, mlstm_norm, wom, wol, xattn_norm, wxq, kx, vx, wxo)


def _ffn_kernel(x_ref, fn_ref, wgu_ref, wd_ref, on_ref, o_ref, *, fc):
    x = x_ref[...]
    hn = _rms(x, fn_ref[...]).astype(BF16)
    dff = wd_ref.shape[0]
    n = dff // fc

    def gate_up(j):
        return (_dot(hn, wgu_ref[:, j * fc:(j + 1) * fc]), _dot(hn, wgu_ref[:, dff + j * fc:dff + (j + 1) * fc]))

    acc = x
    g, u = gate_up(0)
    for j in range(n):
        if j + 1 < n:
            g_next, u_next = gate_up(j + 1)
        acc = acc + _dot((g * jax.nn.sigmoid(g) * u).astype(BF16), wd_ref[j * fc:(j + 1) * fc, :])
        if j + 1 < n:
            g, u = g_next, u_next
    o_ref[...] = _rms(acc, on_ref[...])


def _ffn(x2, ffn_norm, wgu, wd, final_norm, tm, fc):
    T, D = x2.shape
    row = pl.BlockSpec((tm, D), lambda i: (i, 0))
    return pl.pallas_call(
        functools.partial(_ffn_kernel, fc=fc),
        grid=(T // tm,),
        in_specs=[row, _const_spec(ffn_norm.shape), _const_spec(wgu.shape), _const_spec(wd.shape),
                  _const_spec(final_norm.shape)],
        out_specs=row,
        out_shape=jax.ShapeDtypeStruct((T, D), F32),
        compiler_params=_cparams(("arbitrary",)),
        name="ffn",
    )(x2, ffn_norm, wgu, wd, final_norm)


def _head_blocks(wmat, per_head, take, put):
    rows = wmat.shape[0]
    w3 = wmat.reshape(rows, MLA_HEADS, per_head)[:, :, take[0]:take[1]]
    out = jnp.zeros((rows, MLA_HEADS, LANES), wmat.dtype)
    out = out.at[:, :, put:put + (take[1] - take[0])].set(w3)
    return out.reshape(rows, MLA_HEADS * LANES)


def _lane_block(wmat, put):
    rows, n = wmat.shape
    return jnp.zeros((rows, LANES), wmat.dtype).at[:, put:put + n].set(wmat)


def kernel(x, mem, positions, attn_norm, w_in, q_norm, w_uq, kv_norm, w_ukv, mlstm_conv, mlstm_gate_bias,
           mlstm_norm, w_out, xattn_norm, mem_norm, w_xq, w_xkv, w_xo, ffn_norm, w_gate_up, w_down,
           final_norm):
    B, S, D = x.shape
    T = B * S
    assert w_in.shape[0] == 1, "single-layer problem: the final norm is fused into the SwiGLU kernel"
    l = 0
    tm = TOKEN_TILE
    assert S % ATTN_Q_TILE == 0 and S % tm == 0 and S % (MLSTM_CHUNK * CHUNKS_PER_STEP) == 0
    x2 = x.reshape(T, D)

    cos_t, sin_t = _rope_tables(positions)
    vone = jnp.zeros((MLA_HEADS, V_ROWS, 1), F32).at[:, V_DIM, 0].set(1.0).reshape(MLA_HEADS * V_ROWS, 1)
    o_cq, o_ckv, o_kr = 0, Q_LORA, Q_LORA + KV_LORA
    o_qk = o_kr + ROPE_DIM
    o_v = o_qk + 2 * MLSTM_WIDTH
    o_o = o_v + MLSTM_WIDTH
    o_g = o_o + MLSTM_WIDTH
    nh = MLSTM_HEADS
    qd = NOPE_DIM + ROPE_DIM
    wi = w_in[l]
    kr1 = wi[:, o_kr:o_kr + HALF_ROPE]
    kr2 = wi[:, o_kr + HALF_ROPE:o_qk]
    kr_a = _lane_block(jnp.concatenate([kr1, kr2], axis=1), NOPE_DIM)
    kr_b = _lane_block(jnp.concatenate([kr2, kr1], axis=1), NOPE_DIM)
    wg = wi[:, o_g:]
    gate_perm = lambda m: jnp.concatenate(
        [m[:, 0:nh], m[:, 2 * nh:3 * nh], m[:, nh:2 * nh], m[:, 3 * nh:4 * nh]], axis=1)
    wa = jnp.concatenate([wi[:, o_cq:o_kr], kr_a, kr_b, _lane_block(gate_perm(wg), 0)], axis=1).astype(BF16)
    wb = wi[:, o_qk:o_g].astype(BF16)
    gbias = _lane_block(gate_perm(mlstm_gate_bias[l][None, :]), 0)
    wq = w_uq[l]
    wq_a = (_head_blocks(wq, qd, (0, NOPE_DIM), 0)
            + _head_blocks(wq, qd, (NOPE_DIM, qd), NOPE_DIM))
    wq_b = (_head_blocks(wq, qd, (NOPE_DIM + HALF_ROPE, qd), NOPE_DIM)
            + _head_blocks(wq, qd, (NOPE_DIM, NOPE_DIM + HALF_ROPE), NOPE_DIM + HALF_ROPE))
    wqt = jnp.concatenate([wq_a, wq_b], axis=1).T.astype(BF16)
    wkv = w_ukv[l]
    kvd = NOPE_DIM + V_DIM
    wk = _head_blocks(wkv, kvd, (0, NOPE_DIM), 0).astype(BF16)
    wv3 = wkv.reshape(KV_LORA, MLA_HEADS, kvd)[:, :, NOPE_DIM:]
    wvt = jnp.zeros((MLA_HEADS, V_ROWS, KV_LORA), F32).at[:, :V_DIM, :].set(
        wv3.transpose(1, 2, 0)).reshape(MLA_HEADS * V_ROWS, KV_LORA).astype(BF16)
    wo = w_out[l]
    wom = wo[:MLA_HEADS * V_DIM].astype(BF16)
    wol = wo[MLA_HEADS * V_DIM:].astype(BF16)

    kx, vx = _mem_kv(mem, mem_norm[l][None, :], w_xkv[l].astype(BF16))
    qt, k, vt, qm, kt, zv, zo, gc, gr = _in_proj(
        x2, cos_t, sin_t, attn_norm[l][None, :], wa, wb, q_norm[l][None, :], wqt, kv_norm[l][None, :],
        wk, wvt, gbias, vone, mlstm_conv[l], tm, S)
    ya = _mla_attention(qt, k.reshape(B, S, MLA_HEADS * LANES), vt, B, ATTN_Q_TILE, ATTN_K_CHUNK,
                        ATTN_STEPS_PER_ITER)
    hf, hb = _mlstm(qm.reshape(B, S, -1), kt, zv.reshape(B, S, -1), gc.reshape(B, S, -1), gr)
    x2 = _mix_xattn(x2, ya.reshape(T, -1), hf.reshape(T, -1), hb.reshape(T, -1), zo,
                    mlstm_norm[l][None, :], wom, wol, xattn_norm[l][None, :], w_xq[l].astype(BF16),
                    kx, vx, w_xo[l].astype(BF16), tm, S)
    y = _ffn(x2, ffn_norm[l][None, :], w_gate_up[l].astype(BF16), w_down[l].astype(BF16), final_norm[None, :],
             tm, FFN_CHUNK)
    return y.reshape(B, S, D)
```

```python
import functools

import numpy as np
import jax
import jax.numpy as jnp
from jax import lax
from jax.experimental import pallas as pl
from jax.experimental.pallas import tpu as pltpu

F32 = jnp.float32
BF16 = jnp.bfloat16

EPS = 1e-6
MLA_HEADS = 8
Q_LORA = 256
KV_LORA = 128
NOPE_DIM = 64
ROPE_DIM = 32
V_DIM = 64
ROPE_THETA = 10000.0
MLSTM_HEADS = 4
MLSTM_HEAD_DIM = 128
MLSTM_WIDTH = MLSTM_HEADS * MLSTM_HEAD_DIM
MLSTM_CHUNK = 128
X_HEADS = 4
LANES = 128
HALF_ROPE = ROPE_DIM // 2

VMEM_LIMIT = 56 * 1024 * 1024

TOKEN_TILE = 512
FFN_CHUNK = 256
ATTN_Q_TILE = 1024
ATTN_K_CHUNK = 512
ATTN_STEPS_PER_ITER = 4


def _cparams(sem):
    return pltpu.CompilerParams(dimension_semantics=sem, vmem_limit_bytes=VMEM_LIMIT)


def _rms(x, g):
    ms = jnp.mean(x * x, axis=-1, keepdims=True)
    return x * lax.rsqrt(ms + EPS) * g


def _dot(a, b):
    return jnp.dot(a, b, preferred_element_type=F32)


def _dot_nt(a, b):
    return lax.dot_general(a, b, (((1,), (1,)), ((), ())), preferred_element_type=F32)


def _const_spec(shape):
    return pl.BlockSpec(shape, lambda *_: (0,) * len(shape))


def _mem_kv_kernel(mem_ref, g_ref, w_ref, k_ref, v_ref):
    d = mem_ref.shape[-1]
    mn = _rms(mem_ref[0], g_ref[...]).astype(BF16)
    kv = _dot(mn, w_ref[...])
    k_ref[0] = kv[:, :d].astype(BF16)
    v_ref[0] = kv[:, d:].astype(BF16)


def _mem_kv(mem, mem_norm, w_xkv):
    B, M, D = mem.shape
    return pl.pallas_call(
        _mem_kv_kernel,
        grid=(B,),
        in_specs=[pl.BlockSpec((1, M, D), lambda b: (b, 0, 0)),
                  _const_spec((1, D)), _const_spec((D, 2 * D))],
        out_specs=[pl.BlockSpec((1, M, D), lambda b: (b, 0, 0))] * 2,
        out_shape=[jax.ShapeDtypeStruct((B, M, D), BF16)] * 2,
        compiler_params=_cparams(("arbitrary",)),
        name="mem_kv",
    )(mem, mem_norm, w_xkv)


def _rope_kernel(ang_ref, cos_ref, sin_ref):
    ang = ang_ref[...]
    cos_ref[...] = jnp.cos(ang)
    sin_ref[...] = jnp.sin(ang)


def _rope_tables(positions):
    T = positions.size
    inv = ROPE_THETA ** (-jnp.arange(0, ROPE_DIM, 2, dtype=F32) / ROPE_DIM)
    ang = inv[:, None] * positions.astype(F32).reshape(1, T)
    bc = min(T, 8192)
    spec = pl.BlockSpec((HALF_ROPE, bc), lambda i: (0, i))
    return pl.pallas_call(
        _rope_kernel,
        grid=(T // bc,),
        in_specs=[spec],
        out_specs=[spec, spec],
        out_shape=[jax.ShapeDtypeStruct(ang.shape, F32)] * 2,
        compiler_params=_cparams(("arbitrary",)),
        name="rope_tab",
    )(ang)


V_ROWS = 80
HALO = 16
N_DIRS = 2
N_GATE_COLS = N_DIRS * MLSTM_HEADS


def _log_sigmoid(x):
    return jnp.minimum(x, 0.0) - jnp.log(1.0 + jnp.exp(-jnp.abs(x)))


def _chunk_scan(x, op, ident):
    n = x.shape[1]
    rows = lax.broadcasted_iota(jnp.int32, x.shape, 0)
    lanes = lax.broadcasted_iota(jnp.int32, x.shape, 1)
    fwd = rows < MLSTM_HEADS
    k = 1
    while k < n:
        down = pltpu.roll(x, k, axis=1)
        up = pltpu.roll(x, n - k, axis=1)
        shifted = jnp.where(fwd, jnp.where(lanes >= k, down, ident), jnp.where(lanes < n - k, up, ident))
        x = op(x, shifted)
        k *= 2
    return x


def _inproj_kernel(x_ref, xp_ref, xn_ref, cost_ref, sint_ref, an_ref, wa_ref, wb_ref, qn_ref,
                   wqa_ref, wqb_ref, kvn_ref, wk_ref, wvt_ref, gb_ref, vone_ref, cw_ref,
                   qt_ref, k_ref, vt_ref, qm_ref, kt_ref, zv_ref, zo_ref, gc_ref, gr_ref,
                   *, q_scale, k_scale, tiles_per_seq):
    hw = MLA_HEADS * LANES
    tm = x_ref.shape[0]
    L = MLSTM_CHUNK
    w = MLSTM_WIDTH
    i = pl.program_id(0)
    first = (i % tiles_per_seq) == 0
    last = (i % tiles_per_seq) == tiles_per_seq - 1
    x_ext = jnp.concatenate([xp_ref[...], x_ref[...], xn_ref[...]], axis=0)
    h_ext = _rms(x_ext, an_ref[...]).astype(BF16)
    h = h_ext[HALO:HALO + tm]
    za = _dot(h, wa_ref[...])
    cq = za[:, :Q_LORA]
    ckv = za[:, Q_LORA:Q_LORA + KV_LORA]
    kra = za[:, Q_LORA + KV_LORA:Q_LORA + KV_LORA + LANES]
    krb = za[:, Q_LORA + KV_LORA + LANES:Q_LORA + KV_LORA + 2 * LANES]

    gt = (za[:, Q_LORA + KV_LORA + 2 * LANES:] + gb_ref[...]).T
    ng = N_GATE_COLS
    pad = jnp.zeros((L - ng, L), F32)
    for c in range(tm // L):
        cs = slice(c * L, (c + 1) * L)
        ic = gt[0:ng, cs]
        logf = _log_sigmoid(gt[ng:2 * ng, cs])
        bc = _chunk_scan(logf, jnp.add, 0.0)
        uc = ic - bc
        cmc = _chunk_scan(uc, jnp.maximum, -jnp.inf)
        ec = jnp.exp(uc - jnp.max(uc, axis=1, keepdims=True))
        gr_ref[0:ng, cs] = uc
        gr_ref[ng:2 * ng, cs] = ec
        gc_ref[cs, 0:LANES] = jnp.concatenate([cmc, pad], axis=0).T
        gc_ref[cs, LANES:2 * LANES] = jnp.concatenate([bc, pad], axis=0).T

    n_ext = tm + 2 * HALO
    cw = cw_ref[...]

    def conv_silu(cols):
        z = _dot(h_ext, wb_ref[:, cols])
        zc = z[HALO:HALO + tm]
        z_all = jnp.concatenate([jnp.where(first, 0.0, z[:HALO]), zc, jnp.where(last, 0.0, z[HALO + tm:])],
                                axis=0)
        zp = pltpu.roll(z_all, 1, axis=0)[HALO:HALO + tm]
        zn = pltpu.roll(z_all, n_ext - 1, axis=0)[HALO:HALO + tm]
        y = zp * cw[0:1, cols] + zc * cw[1:2, cols]
        y = y + zn * cw[2:3, cols]
        return y * jax.nn.sigmoid(y)

    qm_ref[...] = conv_silu(slice(0, w)).astype(BF16)
    kt_ref[...] = (conv_silu(slice(w, 2 * w)) * k_scale).T.astype(BF16)

    cqn = _rms(cq, qn_ref[...]).astype(BF16)
    ckvn = _rms(ckv, kvn_ref[...]).astype(BF16)
    cos_t = cost_ref[...]
    sin_t = sint_ref[...]
    cos2 = jnp.concatenate([cos_t, cos_t], axis=0)
    sin2 = jnp.concatenate([-sin_t, sin_t], axis=0)
    qd = NOPE_DIM + ROPE_DIM
    qa_t = _dot_nt(wqa_ref[...], cqn)
    qb_t = _dot_nt(wqb_ref[...], cqn)
    zpad = jnp.zeros((LANES - qd, tm), BF16)
    for hd in range(MLA_HEADS):
        nope = qa_t[hd * qd:hd * qd + NOPE_DIM] * q_scale
        rope = (qa_t[hd * qd + NOPE_DIM:(hd + 1) * qd] * (cos2 * q_scale)
                + qb_t[hd * ROPE_DIM:(hd + 1) * ROPE_DIM] * (sin2 * q_scale))
        qt_ref[hd * LANES:hd * LANES + NOPE_DIM, :] = nope.astype(BF16)
        qt_ref[hd * LANES + NOPE_DIM:hd * LANES + qd, :] = rope.astype(BF16)
        qt_ref[hd * LANES + qd:(hd + 1) * LANES, :] = zpad
    kr_t = kra.T[NOPE_DIM:qd] * cos2 + krb.T[NOPE_DIM:qd] * sin2
    krot = jnp.concatenate([jnp.zeros((NOPE_DIM, tm), F32), kr_t, jnp.zeros((LANES - qd, tm), F32)], axis=0).T
    k_ref[...] = _dot(jnp.concatenate([ckvn, krot.astype(BF16)], axis=1), wk_ref[...]).astype(BF16)
    vt_ref[...] = (_dot_nt(wvt_ref[...], ckvn) + vone_ref[...]).astype(BF16)

    zvo = _dot(h, wb_ref[:, 2 * w:])
    zv_ref[...] = zvo[:, :w].astype(BF16)
    zo_ref[...] = zvo[:, w:].astype(BF16)


def _in_proj(x2, cos_t, sin_t, attn_norm, wa, wb, q_norm, wqa, wqb, kv_norm, wk, wvt, gbias, vone, conv_w,
             tm, S):
    T, D = x2.shape
    hw = MLA_HEADS * LANES
    vr = MLA_HEADS * V_ROWS
    w = MLSTM_WIDTH
    per = tm // HALO
    nhb = T // HALO
    row = lambda n: pl.BlockSpec((tm, n), lambda i: (i, 0))
    col = lambda n: pl.BlockSpec((n, tm), lambda i: (0, i))
    prev = pl.BlockSpec((HALO, D), lambda i: (jnp.maximum(i * per - 1, 0), 0))
    nxt = pl.BlockSpec((HALO, D), lambda i: (jnp.minimum((i + 1) * per, nhb - 1), 0))
    out_specs = [col(hw), row(hw), col(vr), row(w), col(w), row(w), row(w), row(2 * LANES),
                 col(2 * N_GATE_COLS)]
    out_shape = [jax.ShapeDtypeStruct((hw, T), BF16), jax.ShapeDtypeStruct((T, hw), BF16),
                 jax.ShapeDtypeStruct((vr, T), BF16),
                 jax.ShapeDtypeStruct((T, w), BF16), jax.ShapeDtypeStruct((w, T), BF16),
                 jax.ShapeDtypeStruct((T, w), BF16), jax.ShapeDtypeStruct((T, w), BF16),
                 jax.ShapeDtypeStruct((T, 2 * LANES), F32), jax.ShapeDtypeStruct((2 * N_GATE_COLS, T), F32)]
    q_scale = float((NOPE_DIM + ROPE_DIM) ** -0.5 * np.log2(np.e))
    consts = (attn_norm, wa, wb, q_norm, wqa, wqb, kv_norm, wk, wvt, gbias, vone, conv_w)
    return pl.pallas_call(
        functools.partial(_inproj_kernel, q_scale=q_scale, k_scale=float(MLSTM_HEAD_DIM ** -0.5),
                          tiles_per_seq=S // tm),
        grid=(T // tm,),
        in_specs=[row(D), prev, nxt, col(HALF_ROPE), col(HALF_ROPE)]
        + [_const_spec(c.shape) for c in consts],
        out_specs=out_specs,
        out_shape=out_shape,
        compiler_params=_cparams(("arbitrary",)),
        name="in_proj",
    )(x2, x2, x2, cos_t, sin_t, *consts)


HEADS_PER_STEP = 4
Q_SUB = 256


def _mla_attn_kernel(qt_ref, qtn_ref, k_ref, vt_ref, o_ref, s_ref, mc_ref, acc_ref, m_ref,
                     *, tk, steps_per_iter):
    S = k_ref.shape[1]
    tq = qt_ref.shape[1]
    nk = S // tk
    assert steps_per_iter % 2 == 0 and nk % steps_per_iter == 0
    heads = range(HEADS_PER_STEP)
    qi = pl.program_id(2)
    nsub = tq // Q_SUB
    units = [(h, slice(i * Q_SUB, (i + 1) * Q_SUB)) for h in heads for i in range(nsub)]

    def scores(off, slot, h, qs, wrap=None):
        hr = slice(h * LANES, (h + 1) * LANES)
        qt = qt_ref[hr, qs]
        if wrap is not None:
            qt = jnp.where(wrap, qtn_ref[hr, qs], qt)
        st = _dot(k_ref[0, pl.ds(off, tk), hr], qt)
        s_ref[slot, h, :, qs] = st
        mc_ref[slot, h, :, qs] = jnp.max(st, axis=0, keepdims=True)

    def consume(off, slot, h, qs):
        m_old = m_ref[h, :, qs]
        m_new = jnp.maximum(m_old, mc_ref[slot, h, :, qs])
        alpha = jnp.exp2(m_old - m_new)
        p = jnp.exp2(s_ref[slot, h, :, qs] - m_new).astype(BF16)
        vt = vt_ref[h * V_ROWS:(h + 1) * V_ROWS, pl.ds(off, tk)]
        acc_ref[h, :, qs] = alpha * acc_ref[h, :, qs] + _dot(vt, p)
        m_ref[h, :, qs] = m_new

    @pl.when(qi == 0)
    def _():
        for h, qs in units:
            scores(0, 0, h, qs)

    for h in heads:
        m_ref[h] = jnp.full((1, tq), -jnp.inf, F32)
        acc_ref[h] = jnp.zeros((V_ROWS, tq), F32)

    def step(c, slot, may_wrap):
        if may_wrap:
            wrap = c + 1 == nk
            nxt = jnp.where(wrap, 0, c + 1)
        else:
            wrap, nxt = None, c + 1
        for h, qs in units:
            scores(pl.multiple_of(nxt * tk, tk), 1 - slot, h, qs, wrap)
            consume(pl.multiple_of(c * tk, tk), slot, h, qs)

    def body(jj, carry):
        for u in range(steps_per_iter):
            step(steps_per_iter * jj + u, u % 2, u == steps_per_iter - 1)
        return carry

    lax.fori_loop(0, nk // steps_per_iter, body, 0)
    outs = []
    for h in heads:
        acc = acc_ref[h]
        o_t = acc[0:V_DIM, :] * (1.0 / acc[V_DIM:V_DIM + 1, :])
        outs.append(o_t.T)
    o_ref[0] = jnp.concatenate(outs, axis=-1).astype(BF16)


def _mla_attention(qt, k, vt, B, tq, tk, steps_per_iter):
    S = k.shape[1]
    hp = HEADS_PER_STEP
    nq = S // tq
    return pl.pallas_call(
        functools.partial(_mla_attn_kernel, tk=tk, steps_per_iter=steps_per_iter),
        grid=(B, MLA_HEADS // hp, nq),
        in_specs=[pl.BlockSpec((hp * LANES, tq), lambda b, h, i: (h, b * nq + i)),
                  pl.BlockSpec((hp * LANES, tq), lambda b, h, i: (h, b * nq + jnp.minimum(i + 1, nq - 1))),
                  pl.BlockSpec((1, S, hp * LANES), lambda b, h, i: (b, 0, h)),
                  pl.BlockSpec((hp * V_ROWS, S), lambda b, h, i: (h, b))],
        out_specs=pl.BlockSpec((1, tq, hp * V_DIM), lambda b, h, i: (b, i, h)),
        out_shape=jax.ShapeDtypeStruct((B, S, MLA_HEADS * V_DIM), BF16),
        scratch_shapes=[pltpu.VMEM((2, hp, tk, tq), F32),
                        pltpu.VMEM((2, hp, 1, tq), F32),
                        pltpu.VMEM((hp, V_ROWS, tq), F32),
                        pltpu.VMEM((hp, 1, tq), F32)],
        compiler_params=_cparams(("arbitrary", "arbitrary", "arbitrary")),
        name="mla_attn",
    )(qt, qt, k, vt)


def _lane_bcast(x, j):
    return jnp.broadcast_to(x[:, j:j + 1], x.shape)


CHUNKS_PER_STEP = 8


def _mlstm_kernel(qf_ref, ktf_ref, vf_ref, gcf_ref, grf_ref, qb_ref, ktb_ref, vb_ref, gcb_ref, grb_ref,
                  hf_ref, hb_ref, ct_ref, m_ref):
    @pl.when(pl.program_id(1) == 0)
    def _():
        ct_ref[...] = jnp.zeros_like(ct_ref)
        m_ref[...] = jnp.zeros_like(m_ref)

    L = MLSTM_CHUNK
    dh = MLSTM_HEAD_DIM
    nh = MLSTM_HEADS
    nsub = CHUNKS_PER_STEP
    rows = lax.broadcasted_iota(jnp.int32, (L, L), 0)
    cols = lax.broadcasted_iota(jnp.int32, (L, L), 1)
    lane1 = lax.broadcasted_iota(jnp.int32, (1, LANES), 1)
    dirs = ((qf_ref, ktf_ref, vf_ref, gcf_ref, grf_ref, hf_ref, False),
            (qb_ref, ktb_ref, vb_ref, gcb_ref, grb_ref, hb_ref, True))

    def chunk_rows(rev, sub):
        c = nsub - 1 - sub if rev else sub
        return slice(c * L, (c + 1) * L)

    m_row = m_ref[...]
    gate = []
    for sub in range(nsub):
        per_dir = []
        for d, (_, _, _, gc_ref, gr_ref, _, rev) in enumerate(dirs):
            rs = chunk_rows(rev, sub)
            gc = gc_ref[0, rs, :]
            cm = gc[:, :LANES]
            b = gc[:, LANES:]
            g = jnp.maximum(m_row, cm)
            floor = jnp.exp(-(b + g))
            end = 0 if rev else L - 1
            blast = b[end:end + 1, :]
            m_loc = blast + cm[end:end + 1, :]
            m_new = jnp.maximum(blast + m_row, m_loc)
            a_row = jnp.exp(blast + m_row - m_new)
            c_row = jnp.exp(m_loc - m_new)
            mask = (cols >= rows) if rev else (cols <= rows)
            per_dir.append((g, floor, a_row, c_row, m_new, mask, gr_ref[:, rs], m_row))
        gate.append(per_dir)
        m_row = jnp.where(lane1 < nh, per_dir[0][4], per_dir[1][4])
    m_ref[...] = m_row

    units = [(sub, d, h) for sub in range(nsub) for d in range(N_DIRS) for h in range(nh)]
    q_l, vx_l, sc_l, kte_l = [], [], [], []
    for sub, d, h in units:
        j = d * nh + h
        hs = slice(h * dh, (h + 1) * dh)
        rs = chunk_rows(dirs[d][6], sub)
        q = dirs[d][0][0, rs, hs]
        kt = dirs[d][1][hs, rs]
        gr = gate[sub][d][6]
        q_l.append(q)
        vx_l.append(jnp.concatenate([dirs[d][2][0, rs, hs], jnp.where(cols == j, 1.0, 0.0).astype(BF16)], axis=1))
        sc_l.append(_dot(q, kt))
        kte_l.append((kt.astype(F32) * gr[N_GATE_COLS + j:N_GATE_COLS + j + 1, :]).astype(BF16))
    s_l, a_l = [], []
    for u, (sub, d, h) in enumerate(units):
        j = d * nh + h
        g, _, _, _, _, mask, gr, m_in = gate[sub][d]
        g_rep = _lane_bcast(g, j)
        a_l.append(jnp.exp(m_in[:, j:j + 1] - g_rep))
        p = jnp.exp(jnp.where(mask, gr[j:j + 1, :] - g_rep, -jnp.inf))
        s_l.append((sc_l[u] * p).astype(BF16))
    nd_l = [_dot(s_l[u], vx_l[u]) for u in range(len(units))]
    cl_l = [_dot(kte_l[u], vx_l[u]) for u in range(len(units))]
    mem = [ct_ref[j] for j in range(N_GATE_COLS)]
    for sub in range(nsub):
        base = sub * N_GATE_COLS
        qc = [_dot(q_l[base + j], mem[j].astype(BF16)) for j in range(N_GATE_COLS)]
        for j in range(N_GATE_COLS):
            d, h = divmod(j, nh)
            u = base + j
            _, floor, a_row, c_row, _, _, _, _ = gate[sub][d]
            tot = nd_l[u] + jnp.concatenate([a_l[u], a_l[u]], axis=1) * qc[j]
            r = 1.0 / jnp.maximum(jnp.abs(tot[:, LANES:]), floor)
            rs = chunk_rows(dirs[d][6], sub)
            dirs[d][5][0, rs, h * dh:(h + 1) * dh] = tot[:, :LANES] * _lane_bcast(r, j)
            mem[j] = a_row[:, j:j + 1] * mem[j] + c_row[:, j:j + 1] * cl_l[u]
    for j in range(N_GATE_COLS):
        ct_ref[j] = mem[j]


def _mlstm(qm, kt, zv, gc, gr):
    B, S, w = qm.shape
    L2 = MLSTM_CHUNK * CHUNKS_PER_STEP
    ns = S // L2
    dh = MLSTM_HEAD_DIM
    fwd = lambda b, c: (b, c, 0)
    bwd = lambda b, c: (b, ns - 1 - c, 0)
    fwd_t = lambda b, c: (0, b * ns + c)
    bwd_t = lambda b, c: (0, b * ns + ns - 1 - c)
    ng2 = 2 * N_GATE_COLS

    def specs(im, im_t):
        return [pl.BlockSpec((1, L2, w), im), pl.BlockSpec((w, L2), im_t), pl.BlockSpec((1, L2, w), im),
                pl.BlockSpec((1, L2, 2 * LANES), im), pl.BlockSpec((ng2, L2), im_t)]

    return pl.pallas_call(
        _mlstm_kernel,
        grid=(B, ns),
        in_specs=specs(fwd, fwd_t) + specs(bwd, bwd_t),
        out_specs=[pl.BlockSpec((1, L2, w), fwd), pl.BlockSpec((1, L2, w), bwd)],
        out_shape=[jax.ShapeDtypeStruct((B, S, w), F32)] * 2,
        scratch_shapes=[pltpu.VMEM((N_GATE_COLS, dh, 2 * LANES), F32),
                        pltpu.VMEM((1, LANES), F32)],
        compiler_params=_cparams(("arbitrary", "arbitrary")),
        name="mlstm",
    )(qm, kt, zv, gc, gr, qm, kt, zv, gc, gr)


def _mix_xattn_kernel(x_ref, ya_ref, hf_ref, hb_ref, zo_ref, mn_ref, wom_ref, wol_ref, xn_ref, wxq_ref,
                      kx_ref, vx_ref, wxo_ref, o_ref, *, x_scale):
    dh = MLSTM_HEAD_DIM
    hs = hf_ref[...] + hb_ref[...]
    parts = []
    for h in range(MLSTM_HEADS):
        sl = slice(h * dh, (h + 1) * dh)
        parts.append(_rms(hs[:, sl], mn_ref[:, sl]))
    hn = jnp.concatenate(parts, axis=-1)
    yl = (jax.nn.sigmoid(zo_ref[...].astype(F32)) * hn).astype(BF16)
    x1 = x_ref[...] + _dot(ya_ref[...], wom_ref[...]) + _dot(yl, wol_ref[...])
    hq = _rms(x1, xn_ref[...]).astype(BF16)
    q = (_dot(hq, wxq_ref[...]) * x_scale).astype(BF16)
    xd = q.shape[-1] // X_HEADS
    sls = [slice(h * xd, (h + 1) * xd) for h in range(X_HEADS)]
    ss = [_dot_nt(q[:, sl], kx_ref[0, :, sl]) for sl in sls]
    ps = [jnp.exp2(s - jnp.max(s, axis=-1, keepdims=True)) for s in ss]
    ls = [jnp.sum(p, axis=-1, keepdims=True) for p in ps]
    pv = [_dot(p.astype(BF16), vx_ref[0, :, sl]) for p, sl in zip(ps, sls)]
    o = jnp.concatenate([(a * (1.0 / l)).astype(BF16) for a, l in zip(pv, ls)], axis=-1)
    o_ref[...] = x1 + _dot(o, wxo_ref[...])


def _mix_xattn(x2, ya, hf, hb, zo, mlstm_norm, wom, wol, xattn_norm, wxq, kx, vx, wxo, tm, S):
    T, D = x2.shape
    M = kx.shape[1]
    per_b = S // tm
    row = lambda n: pl.BlockSpec((tm, n), lambda i: (i, 0))
    mem_spec = pl.BlockSpec((1, M, D), lambda i: (i // per_b, 0, 0))
    x_scale = float((D // X_HEADS) ** -0.5 * np.log2(np.e))
    return pl.pallas_call(
        functools.partial(_mix_xattn_kernel, x_scale=x_scale),
        grid=(T // tm,),
        in_specs=[row(D), row(ya.shape[1]), row(hf.shape[1]), row(hb.shape[1]), row(zo.shape[1]),
                  _const_spec(mlstm_norm.shape), _const_spec(wom.shape), _const_spec(wol.shape),
                  _const_spec(xattn_norm.shape), _const_spec(wxq.shape), mem_spec, mem_spec,
                  _const_spec(wxo.shape)],
        out_specs=row(D),
        out_shape=jax.ShapeDtypeStruct((T, D), F32),
        compiler_params=_cparams(("arbitrary",)),
        name="mix_xattn",
    )(x2, ya, hf, hb, zo, mlstm_norm, wom, wol, xattn_norm, wxq, kx, vx, wxo)


def _ffn_kernel(x_ref, fn_ref, wgu_ref, wd_ref, on_ref, o_ref, *, fc):
    x = x_ref[...]
    hn = _rms(x, fn_ref[...]).astype(BF16)
    dff = wd_ref.shape[0]
    n = dff // fc

    def gate_up(j):
        return (_dot(hn, wgu_ref[:, j * fc:(j + 1) * fc]), _dot(hn, wgu_ref[:, dff + j * fc:dff + (j + 1) * fc]))

    acc = x
    g, u = gate_up(0)
    for j in range(n):
        if j + 1 < n:
            g_next, u_next = gate_up(j + 1)
        acc = acc + _dot((g * jax.nn.sigmoid(g) * u).astype(BF16), wd_ref[j * fc:(j + 1) * fc, :])
        if j + 1 < n:
            g, u = g_next, u_next
    o_ref[...] = _rms(acc, on_ref[...])


def _ffn(x2, ffn_norm, wgu, wd, final_norm, tm, fc):
    T, D = x2.shape
    row = pl.BlockSpec((tm, D), lambda i: (i, 0))
    return pl.pallas_call(
        functools.partial(_ffn_kernel, fc=fc),
        grid=(T // tm,),
        in_specs=[row, _const_spec(ffn_norm.shape), _const_spec(wgu.shape), _const_spec(wd.shape),
                  _const_spec(final_norm.shape)],
        out_specs=row,
        out_shape=jax.ShapeDtypeStruct((T, D), F32),
        compiler_params=_cparams(("arbitrary",)),
        name="ffn",
    )(x2, ffn_norm, wgu, wd, final_norm)


def _head_blocks(wmat, per_head, take, put):
    rows = wmat.shape[0]
    w3 = wmat.reshape(rows, MLA_HEADS, per_head)[:, :, take[0]:take[1]]
    out = jnp.zeros((rows, MLA_HEADS, LANES), wmat.dtype)
    out = out.at[:, :, put:put + (take[1] - take[0])].set(w3)
    return out.reshape(rows, MLA_HEADS * LANES)


def _lane_block(wmat, put):
    rows, n = wmat.shape
    return jnp.zeros((rows, LANES), wmat.dtype).at[:, put:put + n].set(wmat)


def kernel(x, mem, positions, attn_norm, w_in, q_norm, w_uq, kv_norm, w_ukv, mlstm_conv, mlstm_gate_bias,
           mlstm_norm, w_out, xattn_norm, mem_norm, w_xq, w_xkv, w_xo, ffn_norm, w_gate_up, w_down,
           final_norm):
    B, S, D = x.shape
    T = B * S
    assert w_in.shape[0] == 1, "single-layer problem: the final norm is fused into the SwiGLU kernel"
    l = 0
    tm = TOKEN_TILE
    assert S % ATTN_Q_TILE == 0 and S % tm == 0 and S % (MLSTM_CHUNK * CHUNKS_PER_STEP) == 0
    x2 = x.reshape(T, D)

    cos_t, sin_t = _rope_tables(positions)
    vone = np.zeros((MLA_HEADS, V_ROWS, 1), np.float32)
    vone[:, V_DIM, 0] = 1.0
    vone = jnp.asarray(vone.reshape(MLA_HEADS * V_ROWS, 1))
    o_cq, o_ckv, o_kr = 0, Q_LORA, Q_LORA + KV_LORA
    o_qk = o_kr + ROPE_DIM
    o_v = o_qk + 2 * MLSTM_WIDTH
    o_o = o_v + MLSTM_WIDTH
    o_g = o_o + MLSTM_WIDTH
    nh = MLSTM_HEADS
    qd = NOPE_DIM + ROPE_DIM
    wi = w_in[l]
    kr1 = wi[:, o_kr:o_kr + HALF_ROPE]
    kr2 = wi[:, o_kr + HALF_ROPE:o_qk]
    kr_a = _lane_block(jnp.concatenate([kr1, kr2], axis=1), NOPE_DIM)
    kr_b = _lane_block(jnp.concatenate([kr2, kr1], axis=1), NOPE_DIM)
    wg = wi[:, o_g:]
    gate_perm = lambda m: jnp.concatenate(
        [m[:, 0:nh], m[:, 2 * nh:3 * nh], m[:, nh:2 * nh], m[:, 3 * nh:4 * nh]], axis=1)
    wa = jnp.concatenate([wi[:, o_cq:o_kr], kr_a, kr_b, _lane_block(gate_perm(wg), 0)], axis=1).astype(BF16)
    wb = wi[:, o_qk:o_g].astype(BF16)
    gbias = _lane_block(gate_perm(mlstm_gate_bias[l][None, :]), 0)
    wq3 = w_uq[l].reshape(Q_LORA, MLA_HEADS, qd)
    wqa = wq3.reshape(Q_LORA, MLA_HEADS * qd).T.astype(BF16)
    wq_swapped = jnp.concatenate([wq3[:, :, NOPE_DIM + HALF_ROPE:], wq3[:, :, NOPE_DIM:NOPE_DIM + HALF_ROPE]], axis=2)
    wqb = wq_swapped.reshape(Q_LORA, MLA_HEADS * ROPE_DIM).T.astype(BF16)
    wkv = w_ukv[l]
    kvd = NOPE_DIM + V_DIM
    rope_lanes = np.arange(NOPE_DIM, qd)
    place = np.zeros((LANES, MLA_HEADS, LANES), np.float32)
    place[rope_lanes, :, rope_lanes] = 1.0
    wk = jnp.concatenate([_head_blocks(wkv, kvd, (0, NOPE_DIM), 0),
                          jnp.asarray(place.reshape(LANES, MLA_HEADS * LANES))], axis=0).astype(BF16)
    wv3 = wkv.reshape(KV_LORA, MLA_HEADS, kvd)[:, :, NOPE_DIM:]
    wvt = jnp.zeros((MLA_HEADS, V_ROWS, KV_LORA), F32).at[:, :V_DIM, :].set(
        wv3.transpose(1, 2, 0)).reshape(MLA_HEADS * V_ROWS, KV_LORA).astype(BF16)
    wo = w_out[l]
    wom = wo[:MLA_HEADS * V_DIM].astype(BF16)
    wol = wo[MLA_HEADS * V_DIM:].astype(BF16)

    kx, vx = _mem_kv(mem, mem_norm[l][None, :], w_xkv[l].astype(BF16))
    qt, k, vt, qm, kt, zv, zo, gc, gr = _in_proj(
        x2, cos_t, sin_t, attn_norm[l][None, :], wa, wb, q_norm[l][None, :], wqa, wqb, kv_norm[l][None, :],
        wk, wvt, gbias, vone, mlstm_conv[l], tm, S)
    ya = _mla_attention(qt, k.reshape(B, S, MLA_HEADS * LANES), vt, B, ATTN_Q_TILE, ATTN_K_CHUNK,
                        ATTN_STEPS_PER_ITER)
    hf, hb = _mlstm(qm.reshape(B, S, -1), kt, zv.reshape(B, S, -1), gc.reshape(B, S, -1), gr)
    x2 = _mix_xattn(x2, ya.reshape(T, -1), hf.reshape(T, -1), hb.reshape(T, -1), zo,
                    mlstm_norm[l][None, :], wom, wol, xattn_norm[l][None, :], w_xq[l].astype(BF16),
                    kx, vx, w_xo[l].astype(BF16), tm, S)
    y = _ffn(x2, ffn_norm[l][None, :], w_gate_up[l].astype(BF16), w_down[l].astype(BF16), final_norm[None, :],
             tm, FFN_CHUNK)
    return y.reshape(B, S, D)
```

```python
import functools

import numpy as np
import jax
import jax.numpy as jnp
from jax import lax
from jax.experimental import pallas as pl
from jax.experimental.pallas import tpu as pltpu

F32 = jnp.float32
BF16 = jnp.bfloat16

EPS = 1e-6
MLA_HEADS = 8
Q_LORA = 256
KV_LORA = 128
NOPE_DIM = 64
ROPE_DIM = 32
V_DIM = 64
ROPE_THETA = 10000.0
MLSTM_HEADS = 4
MLSTM_HEAD_DIM = 128
MLSTM_WIDTH = MLSTM_HEADS * MLSTM_HEAD_DIM
MLSTM_CHUNK = 128
X_HEADS = 4
LANES = 128
HALF_ROPE = ROPE_DIM // 2

VMEM_LIMIT = 56 * 1024 * 1024

TOKEN_TILE = 512
FFN_CHUNK = 256
ATTN_Q_TILE = 1024
ATTN_K_CHUNK = 512
ATTN_STEPS_PER_ITER = 4


def _cparams(sem):
    return pltpu.CompilerParams(dimension_semantics=sem, vmem_limit_bytes=VMEM_LIMIT)


def _rms(x, g):
    ms = jnp.mean(x * x, axis=-1, keepdims=True)
    return x * lax.rsqrt(ms + EPS) * g


def _dot(a, b):
    return jnp.dot(a, b, preferred_element_type=F32)


def _dot_nt(a, b):
    return lax.dot_general(a, b, (((1,), (1,)), ((), ())), preferred_element_type=F32)


def _const_spec(shape):
    return pl.BlockSpec(shape, lambda *_: (0,) * len(shape))


def _mem_kv_kernel(mem_ref, g_ref, w_ref, k_ref, v_ref):
    d = mem_ref.shape[-1]
    mn = _rms(mem_ref[0], g_ref[...]).astype(BF16)
    kv = _dot(mn, w_ref[...])
    k_ref[0] = kv[:, :d].astype(BF16)
    v_ref[0] = kv[:, d:].astype(BF16)


def _mem_kv(mem, mem_norm, w_xkv):
    B, M, D = mem.shape
    return pl.pallas_call(
        _mem_kv_kernel,
        grid=(B,),
        in_specs=[pl.BlockSpec((1, M, D), lambda b: (b, 0, 0)),
                  _const_spec((1, D)), _const_spec((D, 2 * D))],
        out_specs=[pl.BlockSpec((1, M, D), lambda b: (b, 0, 0))] * 2,
        out_shape=[jax.ShapeDtypeStruct((B, M, D), BF16)] * 2,
        compiler_params=_cparams(("arbitrary",)),
        name="mem_kv",
    )(mem, mem_norm, w_xkv)


def _rope_kernel(ang_ref, cos_ref, sin_ref):
    ang = ang_ref[...]
    cos_ref[...] = jnp.cos(ang)
    sin_ref[...] = jnp.sin(ang)


def _rope_tables(positions):
    T = positions.size
    inv = ROPE_THETA ** (-jnp.arange(0, ROPE_DIM, 2, dtype=F32) / ROPE_DIM)
    ang = inv[:, None] * positions.astype(F32).reshape(1, T)
    bc = min(T, 8192)
    spec = pl.BlockSpec((HALF_ROPE, bc), lambda i: (0, i))
    return pl.pallas_call(
        _rope_kernel,
        grid=(T // bc,),
        in_specs=[spec],
        out_specs=[spec, spec],
        out_shape=[jax.ShapeDtypeStruct(ang.shape, F32)] * 2,
        compiler_params=_cparams(("arbitrary",)),
        name="rope_tab",
    )(ang)


V_ROWS = 80
HALO = 16
N_DIRS = 2
N_GATE_COLS = N_DIRS * MLSTM_HEADS


def _log_sigmoid(x):
    return jnp.minimum(x, 0.0) - jnp.log(1.0 + jnp.exp(-jnp.abs(x)))


def _chunk_scan(x, op, ident):
    n = x.shape[1]
    rows = lax.broadcasted_iota(jnp.int32, x.shape, 0)
    lanes = lax.broadcasted_iota(jnp.int32, x.shape, 1)
    fwd = rows < MLSTM_HEADS
    k = 1
    while k < n:
        down = pltpu.roll(x, k, axis=1)
        up = pltpu.roll(x, n - k, axis=1)
        shifted = jnp.where(fwd, jnp.where(lanes >= k, down, ident), jnp.where(lanes < n - k, up, ident))
        x = op(x, shifted)
        k *= 2
    return x


def _inproj_kernel(x_ref, xp_ref, xn_ref, cost_ref, sint_ref, an_ref, wa_ref, wb_ref, qn_ref,
                   wqa_ref, wqb_ref, kvn_ref, wk_ref, wvt_ref, gb_ref, vone_ref, cw_ref,
                   qt_ref, k_ref, vt_ref, qm_ref, kt_ref, zv_ref, zo_ref, gc_ref, gr_ref,
                   *, q_scale, k_scale, tiles_per_seq):
    tm = x_ref.shape[0]
    L = MLSTM_CHUNK
    w = MLSTM_WIDTH
    i = pl.program_id(0)
    first = (i % tiles_per_seq) == 0
    last = (i % tiles_per_seq) == tiles_per_seq - 1
    x_ext = jnp.concatenate([xp_ref[...], x_ref[...], xn_ref[...]], axis=0)
    h_ext = _rms(x_ext, an_ref[...]).astype(BF16)
    h = h_ext[HALO:HALO + tm]
    za = _dot(h, wa_ref[...])
    cq = za[:, :Q_LORA]
    ckv = za[:, Q_LORA:Q_LORA + KV_LORA]
    kra = za[:, Q_LORA + KV_LORA:Q_LORA + KV_LORA + LANES]
    krb = za[:, Q_LORA + KV_LORA + LANES:Q_LORA + KV_LORA + 2 * LANES]

    gt = (za[:, Q_LORA + KV_LORA + 2 * LANES:] + gb_ref[...]).T
    ng = N_GATE_COLS
    pad = jnp.zeros((L - ng, L), F32)
    for c in range(tm // L):
        cs = slice(c * L, (c + 1) * L)
        ic = gt[0:ng, cs]
        logf = _log_sigmoid(gt[ng:2 * ng, cs])
        bc = _chunk_scan(logf, jnp.add, 0.0)
        uc = ic - bc
        cmc = _chunk_scan(uc, jnp.maximum, -jnp.inf)
        ec = jnp.exp(uc - jnp.max(uc, axis=1, keepdims=True))
        gr_ref[0:ng, cs] = uc
        gr_ref[ng:2 * ng, cs] = ec
        gc_ref[cs, 0:LANES] = jnp.concatenate([cmc, pad], axis=0).T
        gc_ref[cs, LANES:2 * LANES] = jnp.concatenate([bc, pad], axis=0).T

    n_ext = tm + 2 * HALO
    cw = cw_ref[...]

    def conv_silu(cols):
        z = _dot(h_ext, wb_ref[:, cols])
        zc = z[HALO:HALO + tm]
        z_all = jnp.concatenate([jnp.where(first, 0.0, z[:HALO]), zc, jnp.where(last, 0.0, z[HALO + tm:])],
                                axis=0)
        zp = pltpu.roll(z_all, 1, axis=0)[HALO:HALO + tm]
        zn = pltpu.roll(z_all, n_ext - 1, axis=0)[HALO:HALO + tm]
        y = zp * cw[0:1, cols] + zc * cw[1:2, cols]
        y = y + zn * cw[2:3, cols]
        return y * jax.nn.sigmoid(y)

    qm_ref[...] = conv_silu(slice(0, w)).astype(BF16)
    kt_ref[...] = (conv_silu(slice(w, 2 * w)) * k_scale).T.astype(BF16)

    cqn = _rms(cq, qn_ref[...]).astype(BF16)
    ckvn = _rms(ckv, kvn_ref[...]).astype(BF16)
    cos_t = cost_ref[...]
    sin_t = sint_ref[...]
    cos2 = jnp.concatenate([cos_t, cos_t], axis=0)
    sin2 = jnp.concatenate([-sin_t, sin_t], axis=0)
    qd = NOPE_DIM + ROPE_DIM
    qa_t = _dot_nt(wqa_ref[...], cqn)
    qb_t = _dot_nt(wqb_ref[...], cqn)
    zpad = jnp.zeros((LANES - qd, tm), BF16)
    for hd in range(MLA_HEADS):
        nope = qa_t[hd * qd:hd * qd + NOPE_DIM] * q_scale
        rope = (qa_t[hd * qd + NOPE_DIM:(hd + 1) * qd] * (cos2 * q_scale)
                + qb_t[hd * ROPE_DIM:(hd + 1) * ROPE_DIM] * (sin2 * q_scale))
        qt_ref[hd * LANES:hd * LANES + NOPE_DIM, :] = nope.astype(BF16)
        qt_ref[hd * LANES + NOPE_DIM:hd * LANES + qd, :] = rope.astype(BF16)
        qt_ref[hd * LANES + qd:(hd + 1) * LANES, :] = zpad
    kr_t = kra.T[NOPE_DIM:qd] * cos2 + krb.T[NOPE_DIM:qd] * sin2
    krot = jnp.concatenate([jnp.zeros((NOPE_DIM, tm), F32), kr_t, jnp.zeros((LANES - qd, tm), F32)], axis=0).T
    k_ref[...] = _dot(jnp.concatenate([ckvn, krot.astype(BF16)], axis=1), wk_ref[...]).astype(BF16)
    vt_ref[...] = (_dot_nt(wvt_ref[...], ckvn) + vone_ref[...]).astype(BF16)

    zvo = _dot(h, wb_ref[:, 2 * w:])
    zv_ref[...] = zvo[:, :w].astype(BF16)
    zo_ref[...] = zvo[:, w:].astype(BF16)


def _in_proj(x2, cos_t, sin_t, attn_norm, wa, wb, q_norm, wqa, wqb, kv_norm, wk, wvt, gbias, vone, conv_w,
             tm, S):
    T, D = x2.shape
    hw = MLA_HEADS * LANES
    vr = MLA_HEADS * V_ROWS
    w = MLSTM_WIDTH
    per = tm // HALO
    nhb = T // HALO
    row = lambda n: pl.BlockSpec((tm, n), lambda i: (i, 0))
    col = lambda n: pl.BlockSpec((n, tm), lambda i: (0, i))
    prev = pl.BlockSpec((HALO, D), lambda i: (jnp.maximum(i * per - 1, 0), 0))
    nxt = pl.BlockSpec((HALO, D), lambda i: (jnp.minimum((i + 1) * per, nhb - 1), 0))
    out_specs = [col(hw), row(hw), col(vr), row(w), col(w), row(w), row(w), row(2 * LANES),
                 col(2 * N_GATE_COLS)]
    out_shape = [jax.ShapeDtypeStruct((hw, T), BF16), jax.ShapeDtypeStruct((T, hw), BF16),
                 jax.ShapeDtypeStruct((vr, T), BF16),
                 jax.ShapeDtypeStruct((T, w), BF16), jax.ShapeDtypeStruct((w, T), BF16),
                 jax.ShapeDtypeStruct((T, w), BF16), jax.ShapeDtypeStruct((T, w), BF16),
                 jax.ShapeDtypeStruct((T, 2 * LANES), F32), jax.ShapeDtypeStruct((2 * N_GATE_COLS, T), F32)]
    q_scale = float((NOPE_DIM + ROPE_DIM) ** -0.5 * np.log2(np.e))
    consts = (attn_norm, wa, wb, q_norm, wqa, wqb, kv_norm, wk, wvt, gbias, vone, conv_w)
    return pl.pallas_call(
        functools.partial(_inproj_kernel, q_scale=q_scale, k_scale=float(MLSTM_HEAD_DIM ** -0.5),
                          tiles_per_seq=S // tm),
        grid=(T // tm,),
        in_specs=[row(D), prev, nxt, col(HALF_ROPE), col(HALF_ROPE)]
        + [_const_spec(c.shape) for c in consts],
        out_specs=out_specs,
        out_shape=out_shape,
        compiler_params=_cparams(("arbitrary",)),
        name="in_proj",
    )(x2, x2, x2, cos_t, sin_t, *consts)


HEADS_PER_STEP = 4
Q_SUB = 256


def _mla_attn_kernel(qt_ref, qtn_ref, k_ref, vt_ref, o_ref, s_ref, mc_ref, acc_ref, m_ref,
                     *, tk, steps_per_iter):
    S = k_ref.shape[1]
    tq = qt_ref.shape[1]
    nk = S // tk
    assert steps_per_iter % 2 == 0 and nk % steps_per_iter == 0
    heads = range(HEADS_PER_STEP)
    qi = pl.program_id(2)
    nsub = tq // Q_SUB
    units = [(h, slice(i * Q_SUB, (i + 1) * Q_SUB)) for h in heads for i in range(nsub)]

    def scores(off, slot, h, qs, wrap=None):
        hr = slice(h * LANES, (h + 1) * LANES)
        qt = qt_ref[hr, qs]
        if wrap is not None:
            qt = jnp.where(wrap, qtn_ref[hr, qs], qt)
        st = _dot(k_ref[0, pl.ds(off, tk), hr], qt)
        s_ref[slot, h, :, qs] = st
        mc_ref[slot, h, :, qs] = jnp.max(st, axis=0, keepdims=True)

    def consume(off, slot, h, qs):
        m_old = m_ref[h, :, qs]
        m_new = jnp.maximum(m_old, mc_ref[slot, h, :, qs])
        alpha = jnp.exp2(m_old - m_new)
        p = jnp.exp2(s_ref[slot, h, :, qs] - m_new).astype(BF16)
        vt = vt_ref[h * V_ROWS:(h + 1) * V_ROWS, pl.ds(off, tk)]
        acc_ref[h, :, qs] = alpha * acc_ref[h, :, qs] + _dot(vt, p)
        m_ref[h, :, qs] = m_new

    @pl.when(qi == 0)
    def _():
        for h, qs in units:
            scores(0, 0, h, qs)

    for h in heads:
        m_ref[h] = jnp.full((1, tq), -jnp.inf, F32)
        acc_ref[h] = jnp.zeros((V_ROWS, tq), F32)

    def step(c, slot, may_wrap):
        if may_wrap:
            wrap = c + 1 == nk
            nxt = jnp.where(wrap, 0, c + 1)
        else:
            wrap, nxt = None, c + 1
        for h, qs in units:
            scores(pl.multiple_of(nxt * tk, tk), 1 - slot, h, qs, wrap)
            consume(pl.multiple_of(c * tk, tk), slot, h, qs)

    def body(jj, carry):
        for u in range(steps_per_iter):
            step(steps_per_iter * jj + u, u % 2, u == steps_per_iter - 1)
        return carry

    lax.fori_loop(0, nk // steps_per_iter, body, 0)
    outs = []
    for h in heads:
        acc = acc_ref[h]
        o_t = acc[0:V_DIM, :] * (1.0 / acc[V_DIM:V_DIM + 1, :])
        outs.append(o_t.T)
    o_ref[0] = jnp.concatenate(outs, axis=-1).astype(BF16)


def _mla_attention(qt, k, vt, B, tq, tk, steps_per_iter):
    S = k.shape[1]
    hp = HEADS_PER_STEP
    nq = S // tq
    return pl.pallas_call(
        functools.partial(_mla_attn_kernel, tk=tk, steps_per_iter=steps_per_iter),
        grid=(B, MLA_HEADS // hp, nq),
        in_specs=[pl.BlockSpec((hp * LANES, tq), lambda b, h, i: (h, b * nq + i)),
                  pl.BlockSpec((hp * LANES, tq), lambda b, h, i: (h, b * nq + jnp.minimum(i + 1, nq - 1))),
                  pl.BlockSpec((1, S, hp * LANES), lambda b, h, i: (b, 0, h)),
                  pl.BlockSpec((hp * V_ROWS, S), lambda b, h, i: (h, b))],
        out_specs=pl.BlockSpec((1, tq, hp * V_DIM), lambda b, h, i: (b, i, h)),
        out_shape=jax.ShapeDtypeStruct((B, S, MLA_HEADS * V_DIM), BF16),
        scratch_shapes=[pltpu.VMEM((2, hp, tk, tq), F32),
                        pltpu.VMEM((2, hp, 1, tq), F32),
                        pltpu.VMEM((hp, V_ROWS, tq), F32),
                        pltpu.VMEM((hp, 1, tq), F32)],
        compiler_params=_cparams(("arbitrary", "arbitrary", "arbitrary")),
        name="mla_attn",
    )(qt, qt, k, vt)


def _lane_bcast(x, j):
    return jnp.broadcast_to(x[:, j:j + 1], x.shape)


CHUNKS_PER_STEP = 8


def _mlstm_kernel(qf_ref, ktf_ref, vf_ref, gcf_ref, grf_ref, qb_ref, ktb_ref, vb_ref, gcb_ref, grb_ref,
                  hf_ref, hb_ref, ct_ref, m_ref):
    @pl.when(pl.program_id(1) == 0)
    def _():
        ct_ref[...] = jnp.zeros_like(ct_ref)
        m_ref[...] = jnp.zeros_like(m_ref)

    L = MLSTM_CHUNK
    dh = MLSTM_HEAD_DIM
    nh = MLSTM_HEADS
    nsub = CHUNKS_PER_STEP
    rows = lax.broadcasted_iota(jnp.int32, (L, L), 0)
    cols = lax.broadcasted_iota(jnp.int32, (L, L), 1)
    lane1 = lax.broadcasted_iota(jnp.int32, (1, LANES), 1)
    dirs = ((qf_ref, ktf_ref, vf_ref, gcf_ref, grf_ref, hf_ref, False),
            (qb_ref, ktb_ref, vb_ref, gcb_ref, grb_ref, hb_ref, True))

    def chunk_rows(rev, sub):
        c = nsub - 1 - sub if rev else sub
        return slice(c * L, (c + 1) * L)

    m_row = m_ref[...]
    gate = []
    for sub in range(nsub):
        per_dir = []
        for d, (_, _, _, gc_ref, gr_ref, _, rev) in enumerate(dirs):
            rs = chunk_rows(rev, sub)
            gc = gc_ref[0, rs, :]
            cm = gc[:, :LANES]
            b = gc[:, LANES:]
            g = jnp.maximum(m_row, cm)
            floor = jnp.exp(-(b + g))
            end = 0 if rev else L - 1
            blast = b[end:end + 1, :]
            m_loc = blast + cm[end:end + 1, :]
            m_new = jnp.maximum(blast + m_row, m_loc)
            a_row = jnp.exp(blast + m_row - m_new)
            c_row = jnp.exp(m_loc - m_new)
            mask = (cols >= rows) if rev else (cols <= rows)
            per_dir.append((g, floor, a_row, c_row, m_new, mask, gr_ref[:, rs], m_row))
        gate.append(per_dir)
        m_row = jnp.where(lane1 < nh, per_dir[0][4], per_dir[1][4])
    m_ref[...] = m_row

    units = [(sub, d, h) for sub in range(nsub) for d in range(N_DIRS) for h in range(nh)]
    q_l, vx_l, sc_l, kte_l = [], [], [], []
    for sub, d, h in units:
        j = d * nh + h
        hs = slice(h * dh, (h + 1) * dh)
        rs = chunk_rows(dirs[d][6], sub)
        q = dirs[d][0][0, rs, hs]
        kt = dirs[d][1][hs, rs]
        gr = gate[sub][d][6]
        q_l.append(q)
        vx_l.append(jnp.concatenate([dirs[d][2][0, rs, hs], jnp.where(cols == j, 1.0, 0.0).astype(BF16)], axis=1))
        sc_l.append(_dot(q, kt))
        kte_l.append((kt.astype(F32) * gr[N_GATE_COLS + j:N_GATE_COLS + j + 1, :]).astype(BF16))
    s_l, a_l = [], []
    for u, (sub, d, h) in enumerate(units):
        j = d * nh + h
        g, _, _, _, _, mask, gr, m_in = gate[sub][d]
        g_rep = _lane_bcast(g, j)
        a_l.append(jnp.exp(m_in[:, j:j + 1] - g_rep))
        p = jnp.exp(jnp.where(mask, gr[j:j + 1, :] - g_rep, -jnp.inf))
        s_l.append((sc_l[u] * p).astype(BF16))
    cl_l = [_dot(kte_l[u], vx_l[u]) for u in range(len(units))]
    aq_l = [(a_l[u] * q_l[u].astype(F32)).astype(BF16) for u in range(len(units))]
    mem = [ct_ref[j] for j in range(N_GATE_COLS)]
    for sub in range(nsub):
        base = sub * N_GATE_COLS
        for j in range(N_GATE_COLS):
            d, h = divmod(j, nh)
            u = base + j
            _, floor, a_row, c_row, _, _, _, _ = gate[sub][d]
            tot = _dot(jnp.concatenate([s_l[u], aq_l[u]], axis=1),
                       jnp.concatenate([vx_l[u], mem[j].astype(BF16)], axis=0))
            r = 1.0 / jnp.maximum(jnp.abs(tot[:, LANES:]), floor)
            rs = chunk_rows(dirs[d][6], sub)
            dirs[d][5][0, rs, h * dh:(h + 1) * dh] = tot[:, :LANES] * _lane_bcast(r, j)
            mem[j] = a_row[:, j:j + 1] * mem[j] + c_row[:, j:j + 1] * cl_l[u]
    for j in range(N_GATE_COLS):
        ct_ref[j] = mem[j]


def _mlstm(qm, kt, zv, gc, gr):
    B, S, w = qm.shape
    L2 = MLSTM_CHUNK * CHUNKS_PER_STEP
    ns = S // L2
    dh = MLSTM_HEAD_DIM
    fwd = lambda b, c: (b, c, 0)
    bwd = lambda b, c: (b, ns - 1 - c, 0)
    fwd_t = lambda b, c: (0, b * ns + c)
    bwd_t = lambda b, c: (0, b * ns + ns - 1 - c)
    ng2 = 2 * N_GATE_COLS

    def specs(im, im_t):
        return [pl.BlockSpec((1, L2, w), im), pl.BlockSpec((w, L2), im_t), pl.BlockSpec((1, L2, w), im),
                pl.BlockSpec((1, L2, 2 * LANES), im), pl.BlockSpec((ng2, L2), im_t)]

    return pl.pallas_call(
        _mlstm_kernel,
        grid=(B, ns),
        in_specs=specs(fwd, fwd_t) + specs(bwd, bwd_t),
        out_specs=[pl.BlockSpec((1, L2, w), fwd), pl.BlockSpec((1, L2, w), bwd)],
        out_shape=[jax.ShapeDtypeStruct((B, S, w), F32)] * 2,
        scratch_shapes=[pltpu.VMEM((N_GATE_COLS, dh, 2 * LANES), F32),
                        pltpu.VMEM((1, LANES), F32)],
        compiler_params=_cparams(("arbitrary", "arbitrary")),
        name="mlstm",
    )(qm, kt, zv, gc, gr, qm, kt, zv, gc, gr)


def _mix_xattn_kernel(x_ref, ya_ref, hf_ref, hb_ref, zo_ref, mn_ref, wom_ref, wol_ref, xn_ref, wxq_ref,
                      kx_ref, vx_ref, wxo_ref, o_ref, *, x_scale):
    dh = MLSTM_HEAD_DIM
    hs = hf_ref[...] + hb_ref[...]
    parts = []
    for h in range(MLSTM_HEADS):
        sl = slice(h * dh, (h + 1) * dh)
        parts.append(_rms(hs[:, sl], mn_ref[:, sl]))
    hn = jnp.concatenate(parts, axis=-1)
    yl = (jax.nn.sigmoid(zo_ref[...].astype(F32)) * hn).astype(BF16)
    x1 = x_ref[...] + _dot(ya_ref[...], wom_ref[...]) + _dot(yl, wol_ref[...])
    hq = _rms(x1, xn_ref[...]).astype(BF16)
    q = (_dot(hq, wxq_ref[...]) * x_scale).astype(BF16)
    xd = q.shape[-1] // X_HEADS
    sls = [slice(h * xd, (h + 1) * xd) for h in range(X_HEADS)]
    ss = [_dot_nt(q[:, sl], kx_ref[0, :, sl]) for sl in sls]
    ps = [jnp.exp2(s - jnp.max(s, axis=-1, keepdims=True)) for s in ss]
    ls = [jnp.sum(p, axis=-1, keepdims=True) for p in ps]
    pv = [_dot(p.astype(BF16), vx_ref[0, :, sl]) for p, sl in zip(ps, sls)]
    o = jnp.concatenate([(a * (1.0 / l)).astype(BF16) for a, l in zip(pv, ls)], axis=-1)
    o_ref[...] = x1 + _dot(o, wxo_ref[...])


def _mix_xattn(x2, ya, hf, hb, zo, mlstm_norm, wom, wol, xattn_norm, wxq, kx, vx, wxo, tm, S):
    T, D = x2.shape
    M = kx.shape[1]
    per_b = S // tm
    row = lambda n: pl.BlockSpec((tm, n), lambda i: (i, 0))
    mem_spec = pl.BlockSpec((1, M, D), lambda i: (i // per_b, 0, 0))
    x_scale = float((D // X_HEADS) ** -0.5 * np.log2(np.e))
    return pl.pallas_call(
        functools.partial(_mix_xattn_kernel, x_scale=x_scale),
        grid=(T // tm,),
        in_specs=[row(D), row(ya.shape[1]), row(hf.shape[1]), row(hb.shape[1]), row(zo.shape[1]),
                  _const_spec(mlstm_norm.shape), _const_spec(wom.shape), _const_spec(wol.shape),
                  _const_spec(xattn_norm.shape), _const_spec(wxq.shape), mem_spec, mem_spec,
                  _const_spec(wxo.shape)],
        out_specs=row(D),
        out_shape=jax.ShapeDtypeStruct((T, D), F32),
        compiler_params=_cparams(("arbitrary",)),
        name="mix_xattn",
    )(x2, ya, hf, hb, zo, mlstm_norm, wom, wol, xattn_norm, wxq, kx, vx, wxo)


def _ffn_kernel(x_ref, fn_ref, wgu_ref, wd_ref, on_ref, o_ref, *, fc):
    x = x_ref[...]
    hn = _rms(x, fn_ref[...]).astype(BF16)
    dff = wd_ref.shape[0]
    n = dff // fc

    def gate_up(j):
        return (_dot(hn, wgu_ref[:, j * fc:(j + 1) * fc]), _dot(hn, wgu_ref[:, dff + j * fc:dff + (j + 1) * fc]))

    acc = x
    g, u = gate_up(0)
    for j in range(n):
        if j + 1 < n:
            g_next, u_next = gate_up(j + 1)
        acc = acc + _dot((g * jax.nn.sigmoid(g) * u).astype(BF16), wd_ref[j * fc:(j + 1) * fc, :])
        if j + 1 < n:
            g, u = g_next, u_next
    o_ref[...] = _rms(acc, on_ref[...])


def _ffn(x2, ffn_norm, wgu, wd, final_norm, tm, fc):
    T, D = x2.shape
    row = pl.BlockSpec((tm, D), lambda i: (i, 0))
    return pl.pallas_call(
        functools.partial(_ffn_kernel, fc=fc),
        grid=(T // tm,),
        in_specs=[row, _const_spec(ffn_norm.shape), _const_spec(wgu.shape), _const_spec(wd.shape),
                  _const_spec(final_norm.shape)],
        out_specs=row,
        out_shape=jax.ShapeDtypeStruct((T, D), F32),
        compiler_params=_cparams(("arbitrary",)),
        name="ffn",
    )(x2, ffn_norm, wgu, wd, final_norm)


def _head_blocks(wmat, per_head, take, put):
    rows = wmat.shape[0]
    w3 = wmat.reshape(rows, MLA_HEADS, per_head)[:, :, take[0]:take[1]]
    out = jnp.zeros((rows, MLA_HEADS, LANES), wmat.dtype)
    out = out.at[:, :, put:put + (take[1] - take[0])].set(w3)
    return out.reshape(rows, MLA_HEADS * LANES)


def _lane_block(wmat, put):
    rows, n = wmat.shape
    return jnp.zeros((rows, LANES), wmat.dtype).at[:, put:put + n].set(wmat)


def kernel(x, mem, positions, attn_norm, w_in, q_norm, w_uq, kv_norm, w_ukv, mlstm_conv, mlstm_gate_bias,
           mlstm_norm, w_out, xattn_norm, mem_norm, w_xq, w_xkv, w_xo, ffn_norm, w_gate_up, w_down,
           final_norm):
    B, S, D = x.shape
    T = B * S
    assert w_in.shape[0] == 1, "single-layer problem: the final norm is fused into the SwiGLU kernel"
    l = 0
    tm = TOKEN_TILE
    assert S % ATTN_Q_TILE == 0 and S % tm == 0 and S % (MLSTM_CHUNK * CHUNKS_PER_STEP) == 0
    x2 = x.reshape(T, D)

    cos_t, sin_t = _rope_tables(positions)
    vone = np.zeros((MLA_HEADS, V_ROWS, 1), np.float32)
    vone[:, V_DIM, 0] = 1.0
    vone = jnp.asarray(vone.reshape(MLA_HEADS * V_ROWS, 1))
    o_cq, o_ckv, o_kr = 0, Q_LORA, Q_LORA + KV_LORA
    o_qk = o_kr + ROPE_DIM
    o_v = o_qk + 2 * MLSTM_WIDTH
    o_o = o_v + MLSTM_WIDTH
    o_g = o_o + MLSTM_WIDTH
    nh = MLSTM_HEADS
    qd = NOPE_DIM + ROPE_DIM
    wi = w_in[l]
    kr1 = wi[:, o_kr:o_kr + HALF_ROPE]
    kr2 = wi[:, o_kr + HALF_ROPE:o_qk]
    kr_a = _lane_block(jnp.concatenate([kr1, kr2], axis=1), NOPE_DIM)
    kr_b = _lane_block(jnp.concatenate([kr2, kr1], axis=1), NOPE_DIM)
    wg = wi[:, o_g:]
    gate_perm = lambda m: jnp.concatenate(
        [m[:, 0:nh], m[:, 2 * nh:3 * nh], m[:, nh:2 * nh], m[:, 3 * nh:4 * nh]], axis=1)
    wa = jnp.concatenate([wi[:, o_cq:o_kr], kr_a, kr_b, _lane_block(gate_perm(wg), 0)], axis=1).astype(BF16)
    wb = wi[:, o_qk:o_g].astype(BF16)
    gbias = _lane_block(gate_perm(mlstm_gate_bias[l][None, :]), 0)
    wq3 = w_uq[l].reshape(Q_LORA, MLA_HEADS, qd)
    wqa = wq3.reshape(Q_LORA, MLA_HEADS * qd).T.astype(BF16)
    wq_swapped = jnp.concatenate([wq3[:, :, NOPE_DIM + HALF_ROPE:], wq3[:, :, NOPE_DIM:NOPE_DIM + HALF_ROPE]], axis=2)
    wqb = wq_swapped.reshape(Q_LORA, MLA_HEADS * ROPE_DIM).T.astype(BF16)
    wkv = w_ukv[l]
    kvd = NOPE_DIM + V_DIM
    rope_lanes = np.arange(NOPE_DIM, qd)
    place = np.zeros((LANES, MLA_HEADS, LANES), np.float32)
    place[rope_lanes, :, rope_lanes] = 1.0
    wk = jnp.concatenate([_head_blocks(wkv, kvd, (0, NOPE_DIM), 0),
                          jnp.asarray(place.reshape(LANES, MLA_HEADS * LANES))], axis=0).astype(BF16)
    wv3 = wkv.reshape(KV_LORA, MLA_HEADS, kvd)[:, :, NOPE_DIM:]
    wvt = jnp.zeros((MLA_HEADS, V_ROWS, KV_LORA), F32).at[:, :V_DIM, :].set(
        wv3.transpose(1, 2, 0)).reshape(MLA_HEADS * V_ROWS, KV_LORA).astype(BF16)
    wo = w_out[l]
    wom = wo[:MLA_HEADS * V_DIM].astype(BF16)
    wol = wo[MLA_HEADS * V_DIM:].astype(BF16)

    kx, vx = _mem_kv(mem, mem_norm[l][None, :], w_xkv[l].astype(BF16))
    qt, k, vt, qm, kt, zv, zo, gc, gr = _in_proj(
        x2, cos_t, sin_t, attn_norm[l][None, :], wa, wb, q_norm[l][None, :], wqa, wqb, kv_norm[l][None, :],
        wk, wvt, gbias, vone, mlstm_conv[l], tm, S)
    ya = _mla_attention(qt, k.reshape(B, S, MLA_HEADS * LANES), vt, B, ATTN_Q_TILE, ATTN_K_CHUNK,
                        ATTN_STEPS_PER_ITER)
    hf, hb = _mlstm(qm.reshape(B, S, -1), kt, zv.reshape(B, S, -1), gc.reshape(B, S, -1), gr)
    x2 = _mix_xattn(x2, ya.reshape(T, -1), hf.reshape(T, -1), hb.reshape(T, -1), zo,
                    mlstm_norm[l][None, :], wom, wol, xattn_norm[l][None, :], w_xq[l].astype(BF16),
                    kx, vx, w_xo[l].astype(BF16), tm, S)
    y = _ffn(x2, ffn_norm[l][None, :], w_gate_up[l].astype(BF16), w_down[l].astype(BF16), final_norm[None, :],
             tm, FFN_CHUNK)
    return y.reshape(B, S, D)
```
